```python
import math
import jax, jax.numpy as jnp
from jax import lax
import numpy as np

D_MODEL = 1024
BATCH = 2
SEQ = 8192
DEPTH = 1
DEC_BATCH = 128
DEC_SEQ = 1
PAST_LEN = 2048
PAGE_SIZE = 128

PLE_DIM = 256
LRU_WIDTH = 512
LRU_BLOCKS = 8
LRU_BLOCK = LRU_WIDTH // LRU_BLOCKS
CONV_WIDTH = 4
ATT_HEADS = 4
ATT_HD = 64
ATT_WIDTH = ATT_HEADS * 2 * ATT_HD
MIX_WIDTH = LRU_WIDTH + ATT_WIDTH
IN_WIDTH = 2 * LRU_WIDTH + 3 * ATT_WIDTH
SPLITS = (LRU_WIDTH, 2 * LRU_WIDTH, 2 * LRU_WIDTH + ATT_WIDTH, 2 * LRU_WIDTH + 2 * ATT_WIDTH)
ROT_DIM = ATT_HD // 4
ROPE_THETA = 500000.0
Q_BLOCK = 128
D_FF = -(-8 * D_MODEL // (3 * 256)) * 256
NEG_INF = -1e30

kernel_name = 'hymba_rglru_diffattn_decode_step'


def _rmsnorm(x, g, eps=1e-6):
    xf = x.astype(jnp.float32)
    y = xf * lax.rsqrt(jnp.mean(xf * xf, axis=-1, keepdims=True) + eps)
    return (y * g.astype(jnp.float32)).astype(x.dtype)


def _rope(t, pos):
    half = ROT_DIM // 2
    inv = ROPE_THETA ** (-jnp.arange(0, ROT_DIM, 2, dtype=jnp.float32) / ROT_DIM)
    ang = pos.astype(jnp.float32)[:, None] * inv[None, :]
    cos = jnp.cos(ang)[None, :, None, :]
    sin = jnp.sin(ang)[None, :, None, :]
    tf = t.astype(jnp.float32)
    t1 = tf[..., :half]
    t2 = tf[..., half:ROT_DIM]
    out = jnp.concatenate([t1 * cos - t2 * sin, t2 * cos + t1 * sin, tf[..., ROT_DIM:]], axis=-1)
    return out.astype(t.dtype)


def _lin_combine(c1, c2):
    a1, b1 = c1
    a2, b2 = c2
    return a1 * a2, a2 * b1 + b2


def _diff_attend(q1, q2, k1, k2, v, q_pos, k_pos, lam):
    scale = ATT_HD ** -0.5
    mask = k_pos[None, :] <= q_pos[:, None]

    def probs(q, k):
        s = jnp.einsum('bqhd,bkhd->bhqk', q.astype(jnp.float32), k.astype(jnp.float32)) * scale
        return jax.nn.softmax(jnp.where(mask, s, NEG_INF), axis=-1)

    w = probs(q1, k1) - lam * probs(q2, k2)
    return jnp.einsum('bhqk,bkhe->bqhe', w, v.astype(jnp.float32))


def _attend(q1, q2, k1, k2, v, q_pos, k_pos, lam):
    b, sq = q1.shape[0], q1.shape[1]
    if sq <= Q_BLOCK or sq % Q_BLOCK != 0:
        return _diff_attend(q1, q2, k1, k2, v, q_pos, k_pos, lam)
    nb = sq // Q_BLOCK

    def split(t):
        return jnp.moveaxis(t.reshape((b, nb, Q_BLOCK) + t.shape[2:]), 1, 0)

    def one(args):
        qa, qb, qp = args
        return _diff_attend(qa, qb, k1, k2, v, qp, k_pos, lam)

    out = lax.map(one, (split(q1), split(q2), q_pos.reshape(nb, Q_BLOCK)))
    return jnp.moveaxis(out, 0, 1).reshape((b, sq) + out.shape[3:])


def _layer(x, p, conv_prev, h_prev, k_past, v_past, lw, lam_init):
    b, s, _ = x.shape
    past = 0 if k_past is None else k_past.shape[1]
    q_pos = past + jnp.arange(s, dtype=jnp.int32)

    h = _rmsnorm(x, lw['norm_pre_mix'])
    u = h @ lw['w_in']
    xl, gl, q, k, v = jnp.split(u, SPLITS, axis=-1)

    xpad = jnp.concatenate([conv_prev.astype(xl.dtype), xl], axis=1)
    xc = lw['conv_b'] + xpad[:, 0:s] * lw['conv_w'][0]
    for j in range(1, CONV_WIDTH):
        xc = xc + xpad[:, j:j + s] * lw['conv_w'][j]
    new_conv = xpad[:, -(CONV_WIDTH - 1):]
    xb = xc.reshape(b, s, LRU_BLOCKS, LRU_BLOCK)
    r = jax.nn.sigmoid(jnp.einsum('bsni,nij->bsnj', xb, lw['lru_w_a']) + lw['lru_b_a']).reshape(b, s, LRU_WIDTH)
    ig = jax.nn.sigmoid(jnp.einsum('bsni,nij->bsnj', xb, lw['lru_w_x']) + lw['lru_b_x']).reshape(b, s, LRU_WIDTH)
    log_a = -8.0 * r.astype(jnp.float32) * jax.nn.softplus(-lw['lru_L'].astype(jnp.float32))
    a = jnp.exp(log_a)
    bt = jnp.sqrt(-jnp.expm1(2.0 * log_a)) * (ig * xc).astype(jnp.float32)
    bt = bt.at[:, 0].add(a[:, 0] * h_prev.astype(jnp.float32))
    _, hs = lax.associative_scan(_lin_combine, (a, bt), axis=1)
    lru_out = hs.astype(x.dtype) * jax.nn.gelu(gl)
    new_h = hs[:, -1].astype(x.dtype)

    q = q.reshape(b, s, ATT_HEADS, 2, ATT_HD)
    k = k.reshape(b, s, ATT_HEADS, 2, ATT_HD)
    q1 = _rope(q[..., 0, :], q_pos)
    q2 = _rope(q[..., 1, :], q_pos)
    k_new = jnp.concatenate([_rope(k[..., 0, :], q_pos), _rope(k[..., 1, :], q_pos)], axis=-1)
    v_new = v.reshape(b, s, ATT_HEADS, 2 * ATT_HD)
    if k_past is None:
        k_all, v_all, k_pos = k_new, v_new, q_pos
    else:
        k_all = jnp.concatenate([k_past.astype(k_new.dtype), k_new], axis=1)
        v_all = jnp.concatenate([v_past.astype(v_new.dtype), v_new], axis=1)
        k_pos = jnp.arange(past + s, dtype=jnp.int32)
    f32 = jnp.float32
    lam = (jnp.exp(jnp.sum(lw['lambda_q1'].astype(f32) * lw['lambda_k1'].astype(f32)))
           - jnp.exp(jnp.sum(lw['lambda_q2'].astype(f32) * lw['lambda_k2'].astype(f32))) + lam_init)
    o = _attend(q1, q2, k_all[..., :ATT_HD], k_all[..., ATT_HD:], v_all, q_pos, k_pos, lam)
    o = _rmsnorm(o, lw['subln_g'], eps=1e-5) * (1.0 - lam_init)
    att_out = o.reshape(b, s, ATT_WIDTH).astype(x.dtype)

    mix = jnp.concatenate([lru_out, att_out], axis=-1) @ lw['w_out']
    x = x + _rmsnorm(mix, lw['norm_post_mix'])

    h = _rmsnorm(x, lw['norm_pre_ffn'])
    f = (jax.nn.silu(h @ lw['ffn_w_gate']) * (h @ lw['ffn_w_up'])) @ lw['ffn_w_down']
    x = x + _rmsnorm(f, lw['norm_post_ffn'])

    gate = jax.nn.sigmoid(x @ lw['ple_w_gate'])
    x = x + _rmsnorm(gate * (p @ lw['ple_w_proj']), lw['ple_norm'])
    return x, k_new, v_new, new_conv, new_h


def setup_inputs(seed: int = 0) -> dict:
    key = jax.random.key(seed)
    ks = jax.random.split(key, 40)
    f32 = jnp.float32
    n_pages = PAST_LEN // PAGE_SIZE
    n_used = DEC_BATCH * n_pages
    n_phys = n_used + max(1, n_used // 4)

    def nrm(k, shape, scale):
        return jax.random.normal(k, shape, f32) * scale

    def gain(k, shape):
        return 1.0 + 0.05 * jax.random.normal(k, shape, f32)

    page_table = jax.random.permutation(ks[0], n_phys)[:n_used].reshape(DEC_BATCH, n_pages).astype(jnp.int32)
    a0 = jax.random.uniform(ks[1], (DEPTH, LRU_WIDTH), f32, 0.9, 0.999)
    lru_L = jnp.log(a0) - jnp.log1p(-a0)
    return {
        'x_prompt': nrm(ks[2], (BATCH, SEQ, D_MODEL), 1.0),
        'x_sample': nrm(ks[3], (DEC_BATCH, DEC_SEQ, D_MODEL), 1.0),
        'cache_k': nrm(ks[4], (DEPTH, n_phys, PAGE_SIZE, ATT_HEADS, 2 * ATT_HD), 1.0),
        'cache_v': nrm(ks[5], (DEPTH, n_phys, PAGE_SIZE, ATT_HEADS, 2 * ATT_HD), 1.0),
        'state_conv': nrm(ks[6], (DEPTH, DEC_BATCH, CONV_WIDTH - 1, LRU_WIDTH), 1.0),
        'state_h': nrm(ks[7], (DEPTH, DEC_BATCH, LRU_WIDTH), 0.5),
        'page_table': page_table,
        'p_prompt': nrm(ks[8], (DEPTH, BATCH, SEQ, PLE_DIM), 1.0),
        'p_sample': nrm(ks[9], (DEPTH, DEC_BATCH, DEC_SEQ, PLE_DIM), 1.0),
        'norm_pre_mix': gain(ks[10], (DEPTH, D_MODEL)),
        'w_in': nrm(ks[11], (DEPTH, D_MODEL, IN_WIDTH), D_MODEL ** -0.5),
        'conv_w': nrm(ks[12], (DEPTH, CONV_WIDTH, LRU_WIDTH), CONV_WIDTH ** -0.5),
        'conv_b': nrm(ks[13], (DEPTH, LRU_WIDTH), 0.02),
        'lru_w_a': nrm(ks[14], (DEPTH, LRU_BLOCKS, LRU_BLOCK, LRU_BLOCK), LRU_BLOCK ** -0.5),
        'lru_b_a': nrm(ks[15], (DEPTH, LRU_BLOCKS, LRU_BLOCK), 0.02),
        'lru_w_x': nrm(ks[16], (DEPTH, LRU_BLOCKS, LRU_BLOCK, LRU_BLOCK), LRU_BLOCK ** -0.5),
        'lru_b_x': nrm(ks[17], (DEPTH, LRU_BLOCKS, LRU_BLOCK), 0.02),
        'lru_L': lru_L,
        'lambda_q1': nrm(ks[18], (DEPTH, ATT_HD), 0.1),
        'lambda_k1': nrm(ks[19], (DEPTH, ATT_HD), 0.1),
        'lambda_q2': nrm(ks[20], (DEPTH, ATT_HD), 0.1),
        'lambda_k2': nrm(ks[21], (DEPTH, ATT_HD), 0.1),
        'subln_g': gain(ks[22], (DEPTH, 2 * ATT_HD)),
        'w_out': nrm(ks[23], (DEPTH, MIX_WIDTH, D_MODEL), MIX_WIDTH ** -0.5),
        'norm_post_mix': gain(ks[24], (DEPTH, D_MODEL)),
        'norm_pre_ffn': gain(ks[25], (DEPTH, D_MODEL)),
        'ffn_w_gate': nrm(ks[26], (DEPTH, D_MODEL, D_FF), D_MODEL ** -0.5),
        'ffn_w_up': nrm(ks[27], (DEPTH, D_MODEL, D_FF), D_MODEL ** -0.5),
        'ffn_w_down': nrm(ks[28], (DEPTH, D_FF, D_MODEL), D_FF ** -0.5),
        'norm_post_ffn': gain(ks[29], (DEPTH, D_MODEL)),
        'ple_w_gate': nrm(ks[30], (DEPTH, D_MODEL, D_MODEL), D_MODEL ** -0.5),
        'ple_w_proj': nrm(ks[31], (DEPTH, PLE_DIM, D_MODEL), PLE_DIM ** -0.5),
        'ple_norm': gain(ks[32], (DEPTH, D_MODEL)),
    }


def reference(x_prompt, x_sample, cache_k, cache_v, state_conv, state_h, page_table, p_prompt, p_sample,
              norm_pre_mix, w_in, conv_w, conv_b, lru_w_a, lru_b_a, lru_w_x, lru_b_x, lru_L,
              lambda_q1, lambda_k1, lambda_q2, lambda_k2, subln_g, w_out, norm_post_mix,
              norm_pre_ffn, ffn_w_gate, ffn_w_up, ffn_w_down, norm_post_ffn,
              ple_w_gate, ple_w_proj, ple_norm):
    n_pages = page_table.shape[1]
    xp, xs = x_prompt, x_sample
    kps, vps, cps, hps, kss, vss, css, hss = [], [], [], [], [], [], [], []
    for i in range(DEPTH):
        lw = {
            'norm_pre_mix': norm_pre_mix[i], 'w_in': w_in[i], 'conv_w': conv_w[i], 'conv_b': conv_b[i],
            'lru_w_a': lru_w_a[i], 'lru_b_a': lru_b_a[i], 'lru_w_x': lru_w_x[i], 'lru_b_x': lru_b_x[i],
            'lru_L': lru_L[i], 'lambda_q1': lambda_q1[i], 'lambda_k1': lambda_k1[i],
            'lambda_q2': lambda_q2[i], 'lambda_k2': lambda_k2[i], 'subln_g': subln_g[i],
            'w_out': w_out[i], 'norm_post_mix': norm_post_mix[i], 'norm_pre_ffn': norm_pre_ffn[i],
            'ffn_w_gate': ffn_w_gate[i], 'ffn_w_up': ffn_w_up[i], 'ffn_w_down': ffn_w_down[i],
            'norm_post_ffn': norm_post_ffn[i], 'ple_w_gate': ple_w_gate[i], 'ple_w_proj': ple_w_proj[i],
            'ple_norm': ple_norm[i],
        }
        lam_init = 0.8 - 0.6 * math.exp(-0.3 * i)
        conv0 = jnp.zeros((xp.shape[0], CONV_WIDTH - 1, LRU_WIDTH), xp.dtype)
        h0 = jnp.zeros((xp.shape[0], LRU_WIDTH), jnp.float32)
        xp, kp, vp, cp, hp = _layer(xp, p_prompt[i], conv0, h0, None, None, lw, lam_init)
        db = xs.shape[0]
        k_past = cache_k[i][page_table].reshape(db, n_pages * PAGE_SIZE, ATT_HEADS, 2 * ATT_HD)
        v_past = cache_v[i][page_table].reshape(db, n_pages * PAGE_SIZE, ATT_HEADS, 2 * ATT_HD)
        xs, ksm, vsm, csm, hsm = _layer(xs, p_sample[i], state_conv[i], state_h[i], k_past, v_past, lw, lam_init)
        kps.append(kp); vps.append(vp); cps.append(cp); hps.append(hp)
        kss.append(ksm); vss.append(vsm); css.append(csm); hss.append(hsm)
    return (xp, xs, jnp.stack(kps), jnp.stack(vps), jnp.stack(cps), jnp.stack(hps),
            jnp.stack(kss), jnp.stack(vss), jnp.stack(css), jnp.stack(hss))
```

```python
import functools
import math

import jax
import jax.numpy as jnp
import numpy as np
from jax import lax
from jax.experimental import pallas as pl
from jax.experimental.pallas import tpu as pltpu

_F32 = jnp.float32
_BF16 = jnp.bfloat16

_ROPE_THETA = 500000.0
_ROT_FRACTION = 4
_NEG_INF = -1e30
_NORM_EPS = 1e-6
_SUBLN_EPS = 1e-5
_LRU_LOG_SCALE = 8.0

_LANES = 128
_SUBLANES = 8
_V7X_VMEM_BYTES = 64 * 1024 * 1024
_VMEM_HEADROOM_BYTES = 8 * 1024 * 1024


def _vmem_limit(estimate_bytes):
    return int(min(max(estimate_bytes, 16 * 1024 * 1024), _V7X_VMEM_BYTES - _VMEM_HEADROOM_BYTES))


def _const_spec(shape):
    return pl.BlockSpec(shape, lambda *_: (0,) * len(shape), pipeline_mode=pl.Buffered(1))


def _dot(a, b):
    return jnp.dot(a, b, preferred_element_type=_F32)


def _dot_nt(a, b):
    return lax.dot_general(a, b, (((1,), (1,)), ((), ())), preferred_element_type=_F32)


def _rmsnorm(x, g, eps):
    ms = jnp.mean(x * x, axis=-1, keepdims=True)
    return x * lax.rsqrt(ms + eps) * g


def _softplus(x):
    return jnp.maximum(x, 0.0) + jnp.log1p(jnp.exp(-jnp.abs(x)))


def _gelu_tanh(x):
    c = math.sqrt(2.0 / math.pi)
    return 0.5 * x * (1.0 + jnp.tanh(c * (x + 0.044715 * (x * x * x))))


def _rope_slab(t, cos, sin_lo, sin_hi, half):
    up = pltpu.roll(t, _LANES - half, axis=1)
    down = pltpu.roll(t, half, axis=1)
    return t * cos + up * sin_lo + down * sin_hi


def _gates_and_coeffs(xc, wgate_ref, bgate_ref, lru_l_ref, width):
    pre = _dot(xc.astype(_BF16), wgate_ref[...]) + bgate_ref[...]
    r = jax.nn.sigmoid(pre[:, :width])
    ig = jax.nn.sigmoid(pre[:, width:])
    log_a = (-_LRU_LOG_SCALE) * r * _softplus(-lru_l_ref[...])
    a = jnp.exp(log_a)
    th = jnp.tanh(log_a)
    b = jnp.sqrt(-2.0 * th / (1.0 - th)) * (ig * xc)
    return a, b


def _premix_prompt_kernel(x_ref, g_ref, win_ref, rope_ref, convw_ref, convb_ref, wgate_ref, bgate_ref,
                          lru_l_ref, q_ref, k_ref, v_ref, kb_ref, vb_ref, lru_ref, conv_ref, h_ref,
                          xbuf, hcar, hs_buf, cosb, sinb, *, width, att_width, half, q_scale):
    i = pl.program_id(1)
    tm = x_ref.shape[0]
    inv = rope_ref[0:1, :]
    sign_lo = rope_ref[1:2, :]
    sign_hi = rope_ref[2:3, :]

    @pl.when(i == 0)
    def _():
        xbuf[0:_SUBLANES, :] = jnp.zeros((_SUBLANES, width), _F32)
        hcar[...] = jnp.zeros_like(hcar)
        ang = lax.broadcasted_iota(jnp.int32, (tm, _LANES), 0).astype(_F32) * inv
        cosb[...] = jnp.cos(ang)
        sinb[...] = jnp.sin(ang)

    h = _rmsnorm(x_ref[...], g_ref[...], _NORM_EPS).astype(_BF16)
    u = _dot(h, win_ref[...])

    base = (i * tm).astype(_F32) * inv
    cos_a, sin_a = jnp.cos(base), jnp.sin(base)
    cos = cos_a * cosb[...] - sin_a * sinb[...]
    sin = sin_a * cosb[...] + cos_a * sinb[...]
    sin_lo, sin_hi = sin * sign_lo, sin * sign_hi

    q0 = 2 * width
    k0 = q0 + att_width
    v0 = k0 + att_width
    for s in range(att_width // _LANES):
        lo = s * _LANES
        qs = _rope_slab(u[:, q0 + lo:q0 + lo + _LANES], cos, sin_lo, sin_hi, half)
        q_ref[:, lo:lo + _LANES] = (qs * q_scale).astype(_BF16)
        ks = _rope_slab(u[:, k0 + lo:k0 + lo + _LANES], cos, sin_lo, sin_hi, half)
        k_ref[:, lo:lo + _LANES] = ks
        kb_ref[:, lo:lo + _LANES] = ks.astype(_BF16)
    v = u[:, v0:v0 + att_width]
    v_ref[...] = v
    vb_ref[...] = v.astype(_BF16)

    xl = u[:, :width]
    gl = u[:, width:q0]
    xbuf[_SUBLANES:_SUBLANES + tm, :] = xl
    taps = convw_ref.shape[0]
    xc = convb_ref[...] + xl * convw_ref[taps - 1:taps, :]
    for j in range(1, taps):
        xc = xc + xbuf[pl.ds(_SUBLANES - j, tm), :] * convw_ref[taps - 1 - j:taps - j, :]
    tail = xbuf[pl.ds(tm + _SUBLANES - (taps - 1), taps - 1), :]
    xbuf[_SUBLANES - (taps - 1):_SUBLANES, :] = tail

    a, b = _gates_and_coeffs(xc, wgate_ref, bgate_ref, lru_l_ref, width)
    groups = tm // _SUBLANES
    a3 = a.reshape(groups, _SUBLANES, width)
    b3 = b.reshape(groups, _SUBLANES, width)
    row = lax.broadcasted_iota(jnp.int32, (groups, _SUBLANES, width), 1)
    step = 1
    while step < _SUBLANES:
        keep = row >= step
        b3 = jnp.where(keep, a3 * pltpu.roll(b3, step, axis=1) + b3, b3)
        a3 = jnp.where(keep, a3 * pltpu.roll(a3, step, axis=1), a3)
        step *= 2
    carry = hcar[...]
    for gidx in range(groups):
        hs_buf[gidx * _SUBLANES:(gidx + 1) * _SUBLANES, :] = b3[gidx] + a3[gidx] * carry
        carry = a3[gidx, _SUBLANES - 1:_SUBLANES, :] * carry + b3[gidx, _SUBLANES - 1:_SUBLANES, :]
    hcar[...] = carry
    hs = hs_buf[...]
    lru_ref[...] = (hs * _gelu_tanh(gl)).astype(_BF16)

    @pl.when(i == pl.num_programs(1) - 1)
    def _():
        conv_ref[...] = tail
        h_ref[...] = carry


def _premix_prompt(x, g, win, rope_const, convw, convb, wgate, bgate, lru_l, *, width, att_width, half,
                   q_scale, tm):
    bsz, seq, d = x.shape
    in_width = win.shape[1]
    taps = convw.shape[0]
    row_spec = lambda w: pl.BlockSpec((None, tm, w), lambda b, i: (b, i, 0))
    out_shape = (
        jax.ShapeDtypeStruct((bsz, seq, att_width), _BF16),
        jax.ShapeDtypeStruct((bsz, seq, att_width), _F32),
        jax.ShapeDtypeStruct((bsz, seq, att_width), _F32),
        jax.ShapeDtypeStruct((bsz, seq, att_width), _BF16),
        jax.ShapeDtypeStruct((bsz, seq, att_width), _BF16),
        jax.ShapeDtypeStruct((bsz, seq, width), _BF16),
        jax.ShapeDtypeStruct((bsz, taps - 1, width), _F32),
        jax.ShapeDtypeStruct((bsz, 1, width), _F32),
    )
    est = (2 * win.size + 2 * wgate.size + 2 * tm * d * 4 + 4 * tm * in_width * 4
           + 2 * tm * (3 * att_width * 2 + 2 * att_width * 4 + width * 2) + 6 * tm * width * 4)
    kern = functools.partial(_premix_prompt_kernel, width=width, att_width=att_width, half=half,
                             q_scale=q_scale)
    return pl.pallas_call(
        kern,
        grid=(bsz, seq // tm),
        in_specs=[row_spec(d), _const_spec(g.shape), _const_spec(win.shape), _const_spec(rope_const.shape),
                  _const_spec(convw.shape), _const_spec(convb.shape), _const_spec(wgate.shape),
                  _const_spec(bgate.shape), _const_spec(lru_l.shape)],
        out_specs=(row_spec(att_width), row_spec(att_width), row_spec(att_width), row_spec(att_width),
                   row_spec(att_width), row_spec(width),
                   pl.BlockSpec((None, taps - 1, width), lambda b, i: (b, 0, 0)),
                   pl.BlockSpec((None, 1, width), lambda b, i: (b, 0, 0))),
        out_shape=out_shape,
        scratch_shapes=[pltpu.VMEM((tm + _SUBLANES, width), _F32), pltpu.VMEM((1, width), _F32),
                        pltpu.VMEM((tm, width), _F32), pltpu.VMEM((tm, _LANES), _F32),
                        pltpu.VMEM((tm, _LANES), _F32)],
        compiler_params=pltpu.CompilerParams(dimension_semantics=("arbitrary", "arbitrary"),
                                             vmem_limit_bytes=_vmem_limit(est)),
        name="premix_prompt",
    )(x, g, win, rope_const, convw, convb, wgate, bgate, lru_l)


def _premix_sample_kernel(x_ref, g_ref, win_ref, rope_ref, convw_ref, convb_ref, wgate_ref, bgate_ref,
                          lru_l_ref, cst_ref, hst_ref, q_ref, k_ref, v_ref, lru_ref, conv_ref, h_ref,
                          *, width, att_width, half, q_scale, pos):
    inv = rope_ref[0:1, :]
    ang = jnp.float32(pos) * inv
    cos, sin = jnp.cos(ang), jnp.sin(ang)
    sin_lo, sin_hi = sin * rope_ref[1:2, :], sin * rope_ref[2:3, :]

    h = _rmsnorm(x_ref[...], g_ref[...], _NORM_EPS).astype(_BF16)
    u = _dot(h, win_ref[...])
    q0 = 2 * width
    k0 = q0 + att_width
    v0 = k0 + att_width
    for s in range(att_width // _LANES):
        lo = s * _LANES
        qs = _rope_slab(u[:, q0 + lo:q0 + lo + _LANES], cos, sin_lo, sin_hi, half)
        q_ref[:, lo:lo + _LANES] = qs * q_scale
        k_ref[:, lo:lo + _LANES] = _rope_slab(u[:, k0 + lo:k0 + lo + _LANES], cos, sin_lo, sin_hi, half)
    v_ref[...] = u[:, v0:v0 + att_width]

    xl = u[:, :width]
    gl = u[:, width:q0]
    taps = convw_ref.shape[0]
    xc = convb_ref[...] + xl * convw_ref[taps - 1:taps, :]
    for j in range(taps - 1):
        xc = xc + cst_ref[:, j * width:(j + 1) * width] * convw_ref[j:j + 1, :]
    for j in range(taps - 2):
        conv_ref[:, j * width:(j + 1) * width] = cst_ref[:, (j + 1) * width:(j + 2) * width]
    conv_ref[:, (taps - 2) * width:(taps - 1) * width] = xl

    a, b = _gates_and_coeffs(xc, wgate_ref, bgate_ref, lru_l_ref, width)
    hs = a * hst_ref[...] + b
    h_ref[...] = hs
    lru_ref[...] = (hs * _gelu_tanh(gl)).astype(_BF16)


def _premix_sample(x, g, win, rope_const, convw, convb, wgate, bgate, lru_l, conv_state, h_state, *,
                   width, att_width, half, q_scale, pos):
    n, d = x.shape
    taps = convw.shape[0]
    out_shape = (
        jax.ShapeDtypeStruct((n, att_width), _F32),
        jax.ShapeDtypeStruct((n, att_width), _F32),
        jax.ShapeDtypeStruct((n, att_width), _F32),
        jax.ShapeDtypeStruct((n, width), _BF16),
        jax.ShapeDtypeStruct((n, (taps - 1) * width), _F32),
        jax.ShapeDtypeStruct((n, width), _F32),
    )
    args = (x, g, win, rope_const, convw, convb, wgate, bgate, lru_l, conv_state, h_state)
    est = 2 * (2 * win.size + 2 * wgate.size) + 8 * n * win.shape[1] * 4
    kern = functools.partial(_premix_sample_kernel, width=width, att_width=att_width, half=half,
                             q_scale=q_scale, pos=pos)
    return pl.pallas_call(
        kern,
        grid=(1,),
        in_specs=[pl.BlockSpec(a.shape, lambda i, nd=a.ndim: (0,) * nd) for a in args],
        out_specs=tuple(pl.BlockSpec(s.shape, lambda i: (0, 0)) for s in out_shape),
        out_shape=out_shape,
        compiler_params=pltpu.CompilerParams(dimension_semantics=("arbitrary",),
                                             vmem_limit_bytes=_vmem_limit(est)),
        name="premix_sample",
    )(*args)


def _lambda_value(lq1_ref, lk1_ref, lq2_ref, lk2_ref, lam_init):
    d1 = jnp.sum(lq1_ref[...] * lk1_ref[...], axis=-1, keepdims=True)
    d2 = jnp.sum(lq2_ref[...] * lk2_ref[...], axis=-1, keepdims=True)
    return jnp.exp(d1) - jnp.exp(d2) + lam_init


def _flash_kernel(q_ref, k_ref, v_ref, lq1_ref, lk1_ref, lq2_ref, lk2_ref, sg_ref, o_ref,
                  m_ref, l_ref, acc_ref, *, lam_init, hd):
    qi = pl.program_id(2)
    t = q_ref.shape[0]
    lane = lax.broadcasted_iota(jnp.int32, (t, _LANES), 1)
    q = q_ref[...]
    zero = jnp.zeros_like(q)
    qs = jnp.concatenate([jnp.where(lane < hd, q, zero), jnp.where(lane >= hd, q, zero)], axis=0)

    m_ref[...] = jnp.full_like(m_ref, _NEG_INF)
    l_ref[...] = jnp.zeros_like(l_ref)
    acc_ref[...] = jnp.zeros_like(acc_ref)

    def update(j, masked):
        start = pl.multiple_of(j * t, t)
        kj = k_ref[pl.ds(start, t), :]
        vj = v_ref[pl.ds(start, t), :]
        s = _dot_nt(qs, kj)
        if masked:
            r = lax.broadcasted_iota(jnp.int32, (t, t), 0)
            c = lax.broadcasted_iota(jnp.int32, (t, t), 1)
            ok = jnp.concatenate([c <= r, c <= r], axis=0)
            s = jnp.where(ok, s, _NEG_INF)
        m_prev = m_ref[...]
        m_next = jnp.maximum(m_prev, jnp.max(s, axis=1, keepdims=True))
        p = jnp.exp(s - pltpu.repeat(m_next, t // _LANES, axis=1))
        alpha = jnp.exp(m_prev - m_next)
        l_ref[...] = alpha * l_ref[...] + jnp.sum(p, axis=1, keepdims=True)
        acc_ref[...] = alpha * acc_ref[...] + _dot(p.astype(_BF16), vj)
        m_ref[...] = m_next

    def body(j, carry):
        update(j, False)
        return carry

    lax.fori_loop(0, qi, body, 0)
    update(qi, True)

    lam = _lambda_value(lq1_ref, lk1_ref, lq2_ref, lk2_ref, lam_init)
    o_all = acc_ref[...] / l_ref[...]
    o = o_all[:t] - lam * o_all[t:]
    o = _rmsnorm(o, sg_ref[...], _SUBLN_EPS) * (1.0 - lam_init)
    o_ref[...] = o.astype(o_ref.dtype)


def _flash_prompt(q, k, v, lq1, lk1, lq2, lk2, sg, *, heads, hd, lam_init, t):
    bsz, seq, att_width = q.shape
    hw = att_width // heads
    assert hw == _LANES and 2 * hd == hw
    q_spec = pl.BlockSpec((None, t, hw), lambda b, h, i: (b, i, h))
    kv_spec = pl.BlockSpec((None, seq, hw), lambda b, h, i: (b, 0, h))
    small = [_const_spec(a.shape) for a in (lq1, lk1, lq2, lk2, sg)]
    est = 2 * 2 * seq * hw * 2 + 8 * t * hw * 4 + 6 * (2 * t) * t * 4 + 3 * 2 * t * _LANES * 4
    kern = functools.partial(_flash_kernel, lam_init=lam_init, hd=hd)
    return pl.pallas_call(
        kern,
        grid=(bsz, heads, seq // t),
        in_specs=[q_spec, kv_spec, kv_spec] + small,
        out_specs=q_spec,
        out_shape=jax.ShapeDtypeStruct((bsz, seq, att_width), _BF16),
        scratch_shapes=[pltpu.VMEM((2 * t, _LANES), _F32), pltpu.VMEM((2 * t, _LANES), _F32),
                        pltpu.VMEM((2 * t, hw), _F32)],
        compiler_params=pltpu.CompilerParams(dimension_semantics=("parallel", "parallel", "arbitrary"),
                                             vmem_limit_bytes=_vmem_limit(est)),
        name="flash_prompt",
    )(q, k, v, lq1, lk1, lq2, lk2, sg)


def _decode_kernel(pt_ref, q_ref, kn_ref, vn_ref, lq1_ref, lk1_ref, lq2_ref, lk2_ref, sg_ref, *rest,
                   lam_init, hd, heads, n_pages):
    del pt_ref
    k_refs = rest[:n_pages]
    v_refs = rest[n_pages:2 * n_pages]
    o_ref = rest[2 * n_pages]
    att_width = q_ref.shape[-1]
    maps = 2 * heads
    r = lax.broadcasted_iota(jnp.int32, (maps, att_width), 0)
    c = lax.broadcasted_iota(jnp.int32, (maps, att_width), 1)
    qmat = jnp.where(c // hd == r, jnp.broadcast_to(q_ref[...], (maps, att_width)), 0.0)
    qmat_b = qmat.astype(_BF16)

    scores = [_dot_nt(qmat_b, kr[...].astype(_BF16)) for kr in k_refs]
    s_new = jnp.sum(qmat * kn_ref[...], axis=1, keepdims=True)
    m = s_new
    for s in scores:
        m = jnp.maximum(m, jnp.max(s, axis=1, keepdims=True))
    p_new = jnp.exp(s_new - m)
    l = p_new
    acc = p_new * vn_ref[...]
    for s, vr in zip(scores, v_refs):
        p = jnp.exp(s - m)
        l = l + jnp.sum(p, axis=1, keepdims=True)
        acc = acc + _dot(p.astype(_BF16), vr[...].astype(_BF16))
    o_all = acc / l

    lam = _lambda_value(lq1_ref, lk1_ref, lq2_ref, lk2_ref, lam_init)
    hw = att_width // heads
    for h in range(heads):
        o = o_all[2 * h:2 * h + 1, h * hw:(h + 1) * hw] - lam * o_all[2 * h + 1:2 * h + 2, h * hw:(h + 1) * hw]
        o = _rmsnorm(o, sg_ref[...], _SUBLN_EPS) * (1.0 - lam_init)
        o_ref[:, h * hw:(h + 1) * hw] = o.astype(o_ref.dtype)


def _decode_sample(page_table, q, k_new, v_new, cache_k, cache_v, lq1, lk1, lq2, lk2, sg, *, heads, hd,
                   lam_init):
    n, att_width = q.shape
    n_pages = page_table.shape[1]
    n_phys, page, _ = cache_k.shape
    row3 = lambda a: a.reshape(n, 1, att_width)
    row_spec = pl.BlockSpec((None, 1, att_width), lambda b, pt: (b, 0, 0))
    small = [pl.BlockSpec(a.shape, lambda b, pt: (0, 0)) for a in (lq1, lk1, lq2, lk2, sg)]
    page_specs = [pl.BlockSpec((None, page, att_width), lambda b, pt, j=j: (pt[b, j], 0, 0))
                  for j in range(n_pages)]
    est = 2 * 2 * n_pages * page * att_width * 4 + 8 * n_pages * page * att_width * 2
    kern = functools.partial(_decode_kernel, lam_init=lam_init, hd=hd, heads=heads, n_pages=n_pages)
    out = pl.pallas_call(
        kern,
        grid_spec=pltpu.PrefetchScalarGridSpec(
            num_scalar_prefetch=1,
            grid=(n,),
            in_specs=[row_spec, row_spec, row_spec] + small + page_specs + page_specs,
            out_specs=row_spec,
        ),
        out_shape=jax.ShapeDtypeStruct((n, 1, att_width), _BF16),
        compiler_params=pltpu.CompilerParams(dimension_semantics=("arbitrary",),
                                             vmem_limit_bytes=_vmem_limit(est)),
        name="decode_sample",
    )(page_table, row3(q), row3(k_new), row3(v_new), lq1, lk1, lq2, lk2, sg,
      *([cache_k] * n_pages), *([cache_v] * n_pages))
    return out.reshape(n, att_width)


def _post_kernel(x_ref, lru_ref, att_ref, p_ref, wo_l_ref, wo_a_ref, g1_ref, g2_ref, wg_ref, wu_ref,
                 wd_ref, g3_ref, wpg_ref, wpp_ref, g4_ref, o_ref):
    mix = _dot(lru_ref[...], wo_l_ref[...]) + _dot(att_ref[...], wo_a_ref[...])
    x = x_ref[...] + _rmsnorm(mix, g1_ref[...], _NORM_EPS)
    h = _rmsnorm(x, g2_ref[...], _NORM_EPS).astype(_BF16)
    gate = _dot(h, wg_ref[...])
    up = _dot(h, wu_ref[...])
    act = (gate * jax.nn.sigmoid(gate) * up).astype(_BF16)
    f = _dot(act, wd_ref[...])
    x = x + _rmsnorm(f, g3_ref[...], _NORM_EPS)
    pg = jax.nn.sigmoid(_dot(x.astype(_BF16), wpg_ref[...]))
    pp = _dot(p_ref[...].astype(_BF16), wpp_ref[...])
    o_ref[...] = x + _rmsnorm(pg * pp, g4_ref[...], _NORM_EPS)


def _post(x, lru, att, p, wo_l, wo_a, g1, g2, wg, wu, wd, g3, wpg, wpp, g4, *, tm):
    n, d = x.shape
    tm = min(tm, n)
    d_ff = wg.shape[1]
    row_spec = lambda w: pl.BlockSpec((tm, w), lambda i: (i, 0))
    weights = (wo_l, wo_a, g1, g2, wg, wu, wd, g3, wpg, wpp, g4)
    est = (sum(2 * w.size for w in weights) + 4 * tm * d * 4 + 2 * tm * (lru.shape[1] + att.shape[1]) * 2
           + 2 * tm * p.shape[1] * 4 + 3 * tm * d_ff * 4 + 6 * tm * d * 4)
    return pl.pallas_call(
        _post_kernel,
        grid=(n // tm,),
        in_specs=[row_spec(d), row_spec(lru.shape[1]), row_spec(att.shape[1]), row_spec(p.shape[1])]
                 + [_const_spec(w.shape) for w in weights],
        out_specs=row_spec(d),
        out_shape=jax.ShapeDtypeStruct((n, d), _F32),
        compiler_params=pltpu.CompilerParams(dimension_semantics=("parallel",),
                                             vmem_limit_bytes=_vmem_limit(est)),
        name="post",
    )(x, lru, att, p, *weights)


def _rope_constants(hd):
    rot = hd // _ROT_FRACTION
    half = rot // 2
    inv = _ROPE_THETA ** (-jnp.arange(0, rot, 2, dtype=_F32) / rot)
    c = np.arange(_LANES) % hd
    inv_lane = jnp.where(jnp.asarray(c < rot), inv[c % half], 0.0)
    sign_lo = np.where(c < half, -1.0, 0.0)
    sign_hi = np.where((c >= half) & (c < rot), 1.0, 0.0)
    rows = [inv_lane, jnp.asarray(sign_lo, _F32), jnp.asarray(sign_hi, _F32)]
    rows += [jnp.zeros((_LANES,), _F32)] * (_SUBLANES - len(rows))
    return jnp.stack(rows).astype(_F32), half


def _block_diag(w):
    blocks, n, _ = w.shape
    eye = jnp.eye(blocks, dtype=w.dtype)
    return jnp.einsum('nij,nm->nimj', w, eye).reshape(blocks * n, blocks * n)


def kernel(x_prompt, x_sample, cache_k, cache_v, state_conv, state_h, page_table, p_prompt, p_sample, norm_pre_mix, w_in, conv_w, conv_b, lru_w_a, lru_b_a, lru_w_x, lru_b_x, lru_L, lambda_q1, lambda_k1, lambda_q2, lambda_k2, subln_g, w_out, norm_post_mix, norm_pre_ffn, ffn_w_gate, ffn_w_up, ffn_w_down, norm_post_ffn, ple_w_gate, ple_w_proj, ple_norm):
    depth = w_in.shape[0]
    bsz, seq, d = x_prompt.shape
    n_dec, dec_seq, _ = x_sample.shape
    assert dec_seq == 1
    _, n_phys, page, heads, hw = cache_k.shape
    hd = hw // 2
    att_width = heads * hw
    width = conv_w.shape[-1]
    n_pages = page_table.shape[1]
    past = n_pages * page
    q_scale = hd ** -0.5
    rope_const, half = _rope_constants(hd)
    row = lambda a: a.reshape(1, -1)

    xp = x_prompt
    xs = x_sample.reshape(n_dec, d)
    outs = [[] for _ in range(8)]
    for i in range(depth):
        lam_init = 0.8 - 0.6 * math.exp(-0.3 * i)
        win = w_in[i].astype(_BF16)
        wgate = jnp.concatenate([_block_diag(lru_w_a[i]), _block_diag(lru_w_x[i])], axis=1).astype(_BF16)
        bgate = jnp.concatenate([lru_b_a[i].reshape(1, -1), lru_b_x[i].reshape(1, -1)], axis=1)
        pre_args = (row(norm_pre_mix[i]), win, rope_const, conv_w[i], row(conv_b[i]), wgate, bgate,
                    row(lru_L[i]))
        lam_args = (row(lambda_q1[i]), row(lambda_k1[i]), row(lambda_q2[i]), row(lambda_k2[i]),
                    row(subln_g[i]))
        wo = w_out[i].astype(_BF16)
        post_args = (wo[:width], wo[width:], row(norm_post_mix[i]), row(norm_pre_ffn[i]),
                     ffn_w_gate[i].astype(_BF16), ffn_w_up[i].astype(_BF16), ffn_w_down[i].astype(_BF16),
                     row(norm_post_ffn[i]), ple_w_gate[i].astype(_BF16), ple_w_proj[i].astype(_BF16),
                     row(ple_norm[i]))
        dims = dict(width=width, att_width=att_width, half=half, q_scale=q_scale)

        q, kp, vp, kb, vb, lru, cp, hp = _premix_prompt(xp, *pre_args, tm=256, **dims)
        att = _flash_prompt(q, kb, vb, *lam_args, heads=heads, hd=hd, lam_init=lam_init, t=512)
        xp = _post(xp.reshape(bsz * seq, d), lru.reshape(bsz * seq, width), att.reshape(bsz * seq, att_width),
                   p_prompt[i].reshape(bsz * seq, -1), *post_args, tm=256).reshape(bsz, seq, d)

        qs, ks, vs, lrus, cs, hs = _premix_sample(
            xs, *pre_args, state_conv[i].reshape(n_dec, -1), state_h[i], pos=past, **dims)
        atts = _decode_sample(page_table, qs, ks, vs, cache_k[i].reshape(n_phys, page, att_width),
                              cache_v[i].reshape(n_phys, page, att_width), *lam_args, heads=heads, hd=hd,
                              lam_init=lam_init)
        xs = _post(xs, lrus, atts, p_sample[i].reshape(n_dec, -1), *post_args, tm=256)

        for lst, val in zip(outs, (kp.reshape(bsz, seq, heads, hw), vp.reshape(bsz, seq, heads, hw), cp,
                                   hp.reshape(bsz, width), ks.reshape(n_dec, 1, heads, hw),
                                   vs.reshape(n_dec, 1, heads, hw), cs.reshape(n_dec, -1, width), hs)):
            lst.append(val)
    return (xp, xs.reshape(n_dec, 1, d)) + tuple(jnp.stack(o) for o in outs)
```

```python
import functools
import math

import jax
import jax.numpy as jnp
import numpy as np
from jax import lax
from jax.experimental import pallas as pl
from jax.experimental.pallas import tpu as pltpu

_F32 = jnp.float32
_BF16 = jnp.bfloat16

_ROPE_THETA = 500000.0
_ROT_FRACTION = 4
_NEG_INF = -1e30
_NORM_EPS = 1e-6
_SUBLN_EPS = 1e-5
_LRU_LOG_SCALE = 8.0
_POST_GROUP_ROWS = 256

_LANES = 128
_SUBLANES = 8
_MXU_WIDTH = 256
_V7X_VMEM_BYTES = 64 * 1024 * 1024
_VMEM_HEADROOM_BYTES = 8 * 1024 * 1024


def _vmem_limit(estimate_bytes):
    return int(min(max(estimate_bytes, 16 * 1024 * 1024), _V7X_VMEM_BYTES - _VMEM_HEADROOM_BYTES))


def _const_spec(shape):
    return pl.BlockSpec(shape, lambda *_: (0,) * len(shape), pipeline_mode=pl.Buffered(1))


def _dot(a, b):
    return jnp.dot(a, b, preferred_element_type=_F32)


def _dot_nt(a, b):
    return lax.dot_general(a, b, (((1,), (1,)), ((), ())), preferred_element_type=_F32)


def _rmsnorm(x, g, eps):
    ms = jnp.mean(x * x, axis=-1, keepdims=True)
    return x * lax.rsqrt(ms + eps) * g


def _softplus(x):
    return jnp.maximum(x, 0.0) + jnp.log1p(jnp.exp(-jnp.abs(x)))


def _gelu_tanh(x):
    c = math.sqrt(2.0 / math.pi)
    return 0.5 * x * (1.0 + jnp.tanh(c * (x + 0.044715 * (x * x * x))))


def _rope_slab(t, cos, sin_lo, sin_hi, half):
    up = pltpu.roll(t, _LANES - half, axis=1)
    down = pltpu.roll(t, half, axis=1)
    return t * cos + up * sin_lo + down * sin_hi


def _gates_and_coeffs(xc, wgate_ref, bgate_ref, lru_l_ref, width):
    pre = _dot(xc.astype(_BF16), wgate_ref[...]) + bgate_ref[...]
    r = jax.nn.sigmoid(pre[:, :width])
    ig = jax.nn.sigmoid(pre[:, width:])
    log_a = (-_LRU_LOG_SCALE) * r * _softplus(-lru_l_ref[...])
    a = jnp.exp(log_a)
    th = jnp.tanh(log_a)
    var = -2.0 * th / (1.0 - th)
    std = jnp.where(var > 0.0, var * lax.rsqrt(var), 0.0)
    b = std * (ig * xc)
    return a, b


def _premix_prompt_kernel(x_ref, g_ref, win_ref, rope_ref, convw_ref, convb_ref, wgate_ref, bgate_ref,
                          lru_l_ref, qt_ref, k_ref, v_ref, kb_ref, vt_ref, lru_ref, conv_ref, h_ref,
                          xbuf, hcar, hs_buf, cosb, sinb, *, width, att_width, half, q_scale):
    i = pl.program_id(1)
    tm = x_ref.shape[0]
    inv = rope_ref[0:1, :]
    sign_lo = rope_ref[1:2, :]
    sign_hi = rope_ref[2:3, :]

    @pl.when(i == 0)
    def _():
        xbuf[0:_SUBLANES, :] = jnp.zeros((_SUBLANES, width), _F32)
        hcar[...] = jnp.zeros_like(hcar)
        ang = lax.broadcasted_iota(jnp.int32, (tm, _LANES), 0).astype(_F32) * inv
        cosb[...] = jnp.cos(ang)
        sinb[...] = jnp.sin(ang)

    h = _rmsnorm(x_ref[...], g_ref[...], _NORM_EPS).astype(_BF16)
    u = _dot(h, win_ref[...])

    base = (i * tm).astype(_F32) * inv
    cos_a, sin_a = jnp.cos(base), jnp.sin(base)
    cos = cos_a * cosb[...] - sin_a * sinb[...]
    sin = sin_a * cosb[...] + cos_a * sinb[...]
    sin_lo, sin_hi = sin * sign_lo, sin * sign_hi

    q0 = 2 * width
    k0 = q0 + att_width
    v0 = k0 + att_width
    for s in range(att_width // _LANES):
        lo = s * _LANES
        qs = _rope_slab(u[:, q0 + lo:q0 + lo + _LANES], cos, sin_lo, sin_hi, half)
        qt_ref[s] = (qs * q_scale).T.astype(_BF16)
        ks = _rope_slab(u[:, k0 + lo:k0 + lo + _LANES], cos, sin_lo, sin_hi, half)
        k_ref[:, lo:lo + _LANES] = ks
        kb_ref[:, lo:lo + _LANES] = ks.astype(_BF16)
        vt_ref[s] = u[:, v0 + lo:v0 + lo + _LANES].T.astype(_BF16)
    v_ref[...] = u[:, v0:v0 + att_width]

    xl = u[:, :width]
    gl = u[:, width:q0]
    xbuf[_SUBLANES:_SUBLANES + tm, :] = xl
    taps = convw_ref.shape[0]
    xc = convb_ref[...] + xl * convw_ref[taps - 1:taps, :]
    for j in range(1, taps):
        xc = xc + xbuf[pl.ds(_SUBLANES - j, tm), :] * convw_ref[taps - 1 - j:taps - j, :]
    tail = xbuf[pl.ds(tm + _SUBLANES - (taps - 1), taps - 1), :]
    xbuf[_SUBLANES - (taps - 1):_SUBLANES, :] = tail

    a, b = _gates_and_coeffs(xc, wgate_ref, bgate_ref, lru_l_ref, width)
    groups = tm // _SUBLANES
    a3 = a.reshape(groups, _SUBLANES, width)
    b3 = b.reshape(groups, _SUBLANES, width)
    row = lax.broadcasted_iota(jnp.int32, (groups, _SUBLANES, width), 1)
    step = 1
    while step < _SUBLANES:
        keep = row >= step
        b3 = jnp.where(keep, a3 * pltpu.roll(b3, step, axis=1) + b3, b3)
        a3 = jnp.where(keep, a3 * pltpu.roll(a3, step, axis=1), a3)
        step *= 2
    carry = hcar[...]
    for gidx in range(groups):
        hs_buf[gidx * _SUBLANES:(gidx + 1) * _SUBLANES, :] = b3[gidx] + a3[gidx] * carry
        carry = a3[gidx, _SUBLANES - 1:_SUBLANES, :] * carry + b3[gidx, _SUBLANES - 1:_SUBLANES, :]
    hcar[...] = carry
    hs = hs_buf[...]
    lru_ref[...] = (hs * _gelu_tanh(gl)).astype(_BF16)

    @pl.when(i == pl.num_programs(1) - 1)
    def _():
        conv_ref[...] = tail
        h_ref[...] = carry


def _premix_prompt(x, g, win, rope_const, convw, convb, wgate, bgate, lru_l, *, width, att_width, half,
                   q_scale, tm):
    bsz, seq, d = x.shape
    in_width = win.shape[1]
    taps = convw.shape[0]
    row_spec = lambda w: pl.BlockSpec((None, tm, w), lambda b, i: (b, i, 0))
    slabs = att_width // _LANES
    head_t_spec = pl.BlockSpec((None, slabs, _LANES, tm), lambda b, i: (b, 0, 0, i))
    out_shape = (
        jax.ShapeDtypeStruct((bsz, slabs, _LANES, seq), _BF16),
        jax.ShapeDtypeStruct((bsz, seq, att_width), _F32),
        jax.ShapeDtypeStruct((bsz, seq, att_width), _F32),
        jax.ShapeDtypeStruct((bsz, seq, att_width), _BF16),
        jax.ShapeDtypeStruct((bsz, slabs, _LANES, seq), _BF16),
        jax.ShapeDtypeStruct((bsz, seq, width), _BF16),
        jax.ShapeDtypeStruct((bsz, taps - 1, width), _F32),
        jax.ShapeDtypeStruct((bsz, 1, width), _F32),
    )
    est = (2 * win.size + 2 * wgate.size + 2 * tm * d * 4 + 4 * tm * in_width * 4
           + 2 * tm * (3 * att_width * 2 + 2 * att_width * 4 + width * 2) + 6 * tm * width * 4)
    kern = functools.partial(_premix_prompt_kernel, width=width, att_width=att_width, half=half,
                             q_scale=q_scale)
    return pl.pallas_call(
        kern,
        grid=(bsz, seq // tm),
        in_specs=[row_spec(d), _const_spec(g.shape), _const_spec(win.shape), _const_spec(rope_const.shape),
                  _const_spec(convw.shape), _const_spec(convb.shape), _const_spec(wgate.shape),
                  _const_spec(bgate.shape), _const_spec(lru_l.shape)],
        out_specs=(head_t_spec, row_spec(att_width), row_spec(att_width), row_spec(att_width),
                   head_t_spec, row_spec(width),
                   pl.BlockSpec((None, taps - 1, width), lambda b, i: (b, 0, 0)),
                   pl.BlockSpec((None, 1, width), lambda b, i: (b, 0, 0))),
        out_shape=out_shape,
        scratch_shapes=[pltpu.VMEM((tm + _SUBLANES, width), _F32), pltpu.VMEM((1, width), _F32),
                        pltpu.VMEM((tm, width), _F32), pltpu.VMEM((tm, _LANES), _F32),
                        pltpu.VMEM((tm, _LANES), _F32)],
        compiler_params=pltpu.CompilerParams(dimension_semantics=("arbitrary", "arbitrary"),
                                             vmem_limit_bytes=_vmem_limit(est)),
        name="premix_prompt",
    )(x, g, win, rope_const, convw, convb, wgate, bgate, lru_l)


def _premix_sample_kernel(x_ref, g_ref, win_ref, rope_ref, convw_ref, convb_ref, wgate_ref, bgate_ref,
                          lru_l_ref, cst_ref, hst_ref, q_ref, k_ref, v_ref, lru_ref, conv_ref, h_ref,
                          *, width, att_width, half, q_scale, pos):
    inv = rope_ref[0:1, :]
    ang = jnp.float32(pos) * inv
    cos, sin = jnp.cos(ang), jnp.sin(ang)
    sin_lo, sin_hi = sin * rope_ref[1:2, :], sin * rope_ref[2:3, :]

    h = _rmsnorm(x_ref[...], g_ref[...], _NORM_EPS).astype(_BF16)
    u = _dot(h, win_ref[...])
    q0 = 2 * width
    k0 = q0 + att_width
    v0 = k0 + att_width
    for s in range(att_width // _LANES):
        lo = s * _LANES
        qs = _rope_slab(u[:, q0 + lo:q0 + lo + _LANES], cos, sin_lo, sin_hi, half)
        q_ref[:, lo:lo + _LANES] = qs * q_scale
        k_ref[:, lo:lo + _LANES] = _rope_slab(u[:, k0 + lo:k0 + lo + _LANES], cos, sin_lo, sin_hi, half)
    v_ref[...] = u[:, v0:v0 + att_width]

    xl = u[:, :width]
    gl = u[:, width:q0]
    taps = convw_ref.shape[0]
    xc = convb_ref[...] + xl * convw_ref[taps - 1:taps, :]
    for j in range(taps - 1):
        xc = xc + cst_ref[:, j * width:(j + 1) * width] * convw_ref[j:j + 1, :]
    for j in range(taps - 2):
        conv_ref[:, j * width:(j + 1) * width] = cst_ref[:, (j + 1) * width:(j + 2) * width]
    conv_ref[:, (taps - 2) * width:(taps - 1) * width] = xl

    a, b = _gates_and_coeffs(xc, wgate_ref, bgate_ref, lru_l_ref, width)
    hs = a * hst_ref[...] + b
    h_ref[...] = hs
    lru_ref[...] = (hs * _gelu_tanh(gl)).astype(_BF16)


def _premix_sample(x, g, win, rope_const, convw, convb, wgate, bgate, lru_l, conv_state, h_state, *,
                   width, att_width, half, q_scale, pos):
    n, d = x.shape
    taps = convw.shape[0]
    out_shape = (
        jax.ShapeDtypeStruct((n, att_width), _F32),
        jax.ShapeDtypeStruct((n, att_width), _F32),
        jax.ShapeDtypeStruct((n, att_width), _F32),
        jax.ShapeDtypeStruct((n, width), _BF16),
        jax.ShapeDtypeStruct((n, (taps - 1) * width), _F32),
        jax.ShapeDtypeStruct((n, width), _F32),
    )
    args = (x, g, win, rope_const, convw, convb, wgate, bgate, lru_l, conv_state, h_state)
    est = 2 * (2 * win.size + 2 * wgate.size) + 8 * n * win.shape[1] * 4
    kern = functools.partial(_premix_sample_kernel, width=width, att_width=att_width, half=half,
                             q_scale=q_scale, pos=pos)
    return pl.pallas_call(
        kern,
        grid=(1,),
        in_specs=[pl.BlockSpec(a.shape, lambda i, nd=a.ndim: (0,) * nd) for a in args],
        out_specs=tuple(pl.BlockSpec(s.shape, lambda i: (0, 0)) for s in out_shape),
        out_shape=out_shape,
        compiler_params=pltpu.CompilerParams(dimension_semantics=("arbitrary",),
                                             vmem_limit_bytes=_vmem_limit(est)),
        name="premix_sample",
    )(*args)


def _lambda_value(lq1_ref, lk1_ref, lq2_ref, lk2_ref, lam_init):
    d1 = jnp.sum(lq1_ref[...] * lk1_ref[...], axis=-1, keepdims=True)
    d2 = jnp.sum(lq2_ref[...] * lk2_ref[...], axis=-1, keepdims=True)
    return jnp.exp(d1) - jnp.exp(d2) + lam_init


def _flash_kernel(qt_ref, k_ref, vt_ref, lq1_ref, lk1_ref, lq2_ref, lk2_ref, sg_ref, o_ref,
                  qs_ref, s0_ref, s1_ref, m_ref, l_ref, acc_ref, *, lam_init, hd, chunk):
    qi = pl.program_id(2)
    t = qt_ref.shape[1]
    feat = lax.broadcasted_iota(jnp.int32, (_LANES, t), 0)
    qt = qt_ref[...]
    zero = jnp.zeros_like(qt)
    qs_ref[:, :t] = jnp.where(feat < hd, qt, zero)
    qs_ref[:, t:] = jnp.where(feat >= hd, qt, zero)

    m_ref[...] = jnp.full_like(m_ref, _NEG_INF)
    l_ref[...] = jnp.zeros_like(l_ref)
    acc_ref[...] = jnp.zeros_like(acc_ref)

    def scores(j, s_ref):
        kj = k_ref[pl.ds(pl.multiple_of(j * t, t), t), :]
        for c in range(2 * t // chunk):
            cols = slice(c * chunk, (c + 1) * chunk)
            s_ref[:, cols] = _dot(kj, qs_ref[:, cols])

    def process(j, s_ref, masked):
        vtj = vt_ref[:, pl.ds(pl.multiple_of(j * t, t), t)]
        for c in range(2 * t // chunk):
            cols = slice(c * chunk, (c + 1) * chunk)
            s = s_ref[:, cols]
            if masked:
                kk = lax.broadcasted_iota(jnp.int32, (t, chunk), 0)
                qq = lax.broadcasted_iota(jnp.int32, (t, chunk), 1) + (c * chunk) % t
                s = jnp.where(kk <= qq, s, _NEG_INF)
            m_prev = m_ref[:, cols]
            m_next = jnp.maximum(m_prev, jnp.max(s, axis=0, keepdims=True))
            p = jnp.exp2(s - m_next)
            alpha = jnp.exp2(m_prev - m_next)
            l_ref[:, cols] = alpha * l_ref[:, cols] + jnp.sum(p, axis=0, keepdims=True)
            acc_ref[:, cols] = alpha * acc_ref[:, cols] + _dot(vtj, p.astype(_BF16))
            m_ref[:, cols] = m_next

    scores(0, s0_ref)

    def body(jj, carry):
        j = 2 * jj
        scores(j + 1, s1_ref)
        process(j, s0_ref, False)
        scores(j + 2, s0_ref)
        process(j + 1, s1_ref, False)
        return carry

    lax.fori_loop(0, qi // 2, body, 0)

    @pl.when(qi % 2 == 0)
    def _():
        process(qi, s0_ref, True)

    @pl.when(qi % 2 == 1)
    def _():
        scores(qi, s1_ref)
        process(qi - 1, s0_ref, False)
        process(qi, s1_ref, True)

    lam = _lambda_value(lq1_ref, lk1_ref, lq2_ref, lk2_ref, lam_init)
    o_all = acc_ref[...] * (1.0 / l_ref[...])
    o = o_all[:, :t] - lam * o_all[:, t:]
    ms = jnp.mean(o * o, axis=0, keepdims=True)
    o = o * lax.rsqrt(ms + _SUBLN_EPS) * sg_ref[...] * (1.0 - lam_init)
    o_ref[...] = o.T.astype(o_ref.dtype)


def _flash_prompt(qt, k, vt, lq1, lk1, lq2, lk2, sg_col, *, hd, lam_init, t):
    bsz, heads, hw, seq = qt.shape
    assert hw == _LANES and 2 * hd == hw
    qt_spec = pl.BlockSpec((None, None, hw, t), lambda b, h, i: (b, h, 0, i))
    k_spec = pl.BlockSpec((None, seq, hw), lambda b, h, i: (b, 0, h))
    vt_spec = pl.BlockSpec((None, None, hw, seq), lambda b, h, i: (b, h, 0, 0))
    o_spec = pl.BlockSpec((None, t, hw), lambda b, h, i: (b, i, h))
    small = [_const_spec(a.shape) for a in (lq1, lk1, lq2, lk2, sg_col)]
    est = 2 * 2 * seq * hw * 2 + 8 * t * hw * 4 + 6 * (2 * t) * t * 4 + 2 * hw * 2 * t * 4
    kern = functools.partial(_flash_kernel, lam_init=lam_init, hd=hd, chunk=_MXU_WIDTH)
    return pl.pallas_call(
        kern,
        grid=(bsz, heads, seq // t),
        in_specs=[qt_spec, k_spec, vt_spec] + small,
        out_specs=o_spec,
        out_shape=jax.ShapeDtypeStruct((bsz, seq, heads * hw), _BF16),
        scratch_shapes=[pltpu.VMEM((hw, 2 * t), _BF16), pltpu.VMEM((t, 2 * t), _F32),
                        pltpu.VMEM((t, 2 * t), _F32), pltpu.VMEM((1, 2 * t), _F32),
                        pltpu.VMEM((1, 2 * t), _F32), pltpu.VMEM((hw, 2 * t), _F32)],
        compiler_params=pltpu.CompilerParams(dimension_semantics=("parallel", "parallel", "arbitrary"),
                                             vmem_limit_bytes=_vmem_limit(est)),
        name="flash_prompt",
    )(qt, k, vt, lq1, lk1, lq2, lk2, sg_col)


def _decode_kernel(pt_ref, q_ref, kn_ref, vn_ref, lq1_ref, lk1_ref, lq2_ref, lk2_ref, sg_ref, *rest,
                   lam_init, hd, heads, n_pages):
    del pt_ref
    k_refs = rest[:n_pages]
    v_refs = rest[n_pages:2 * n_pages]
    o_ref = rest[2 * n_pages]
    att_width = q_ref.shape[-1]
    maps = 2 * heads
    r = lax.broadcasted_iota(jnp.int32, (maps, att_width), 0)
    c = lax.broadcasted_iota(jnp.int32, (maps, att_width), 1)
    qmat = jnp.where(c // hd == r, jnp.broadcast_to(q_ref[...], (maps, att_width)), 0.0)
    qmat_b = qmat.astype(_BF16)

    scores = [_dot_nt(qmat_b, kr[...].astype(_BF16)) for kr in k_refs]
    s_new = jnp.sum(qmat * kn_ref[...], axis=1, keepdims=True)
    m = s_new
    for s in scores:
        m = jnp.maximum(m, jnp.max(s, axis=1, keepdims=True))
    p_new = jnp.exp(s_new - m)
    l = p_new
    acc = p_new * vn_ref[...]
    for s, vr in zip(scores, v_refs):
        p = jnp.exp(s - m)
        l = l + jnp.sum(p, axis=1, keepdims=True)
        acc = acc + _dot(p.astype(_BF16), vr[...].astype(_BF16))
    o_all = acc / l

    lam = _lambda_value(lq1_ref, lk1_ref, lq2_ref, lk2_ref, lam_init)
    hw = att_width // heads
    for h in range(heads):
        o = o_all[2 * h:2 * h + 1, h * hw:(h + 1) * hw] - lam * o_all[2 * h + 1:2 * h + 2, h * hw:(h + 1) * hw]
        o = _rmsnorm(o, sg_ref[...], _SUBLN_EPS) * (1.0 - lam_init)
        o_ref[:, h * hw:(h + 1) * hw] = o.astype(o_ref.dtype)


def _decode_sample(page_table, q, k_new, v_new, cache_k, cache_v, lq1, lk1, lq2, lk2, sg, *, heads, hd,
                   lam_init, page_base):
    n, att_width = q.shape
    n_pages = page_table.shape[1]
    _, page, _ = cache_k.shape
    row3 = lambda a: a.reshape(n, 1, att_width)
    row_spec = pl.BlockSpec((None, 1, att_width), lambda b, pt: (b, 0, 0))
    small = [pl.BlockSpec(a.shape, lambda b, pt: (0, 0)) for a in (lq1, lk1, lq2, lk2, sg)]
    page_specs = [pl.BlockSpec((None, page, att_width), lambda b, pt, j=j: (pt[b, j] + page_base, 0, 0))
                  for j in range(n_pages)]
    est = 2 * 2 * n_pages * page * att_width * 4 + 8 * n_pages * page * att_width * 2
    kern = functools.partial(_decode_kernel, lam_init=lam_init, hd=hd, heads=heads, n_pages=n_pages)
    out = pl.pallas_call(
        kern,
        grid_spec=pltpu.PrefetchScalarGridSpec(
            num_scalar_prefetch=1,
            grid=(n,),
            in_specs=[row_spec, row_spec, row_spec] + small + page_specs + page_specs,
            out_specs=row_spec,
        ),
        out_shape=jax.ShapeDtypeStruct((n, 1, att_width), _BF16),
        compiler_params=pltpu.CompilerParams(dimension_semantics=("arbitrary",),
                                             vmem_limit_bytes=_vmem_limit(est)),
        name="decode_sample",
    )(page_table, row3(q), row3(k_new), row3(v_new), lq1, lk1, lq2, lk2, sg,
      *([cache_k] * n_pages), *([cache_v] * n_pages))
    return out.reshape(n, att_width)


def _post_kernel(x_ref, lru_ref, att_ref, p_ref, wo_l_ref, wo_a_ref, g1_ref, g2_ref, wg_ref, wu_ref,
                 wd_ref, g3_ref, wpg_ref, wpp_ref, g4_ref, o_ref):
    tm = x_ref.shape[0]
    n_groups = max(1, tm // _POST_GROUP_ROWS)
    rows = [slice(r * (tm // n_groups), (r + 1) * (tm // n_groups)) for r in range(n_groups)]
    mix = [_dot(lru_ref[r, :], wo_l_ref[...]) + _dot(att_ref[r, :], wo_a_ref[...]) for r in rows]
    x1, gate, up = [], [], []
    for r, mx in zip(rows, mix):
        x = x_ref[r, :] + _rmsnorm(mx, g1_ref[...], _NORM_EPS)
        h = _rmsnorm(x, g2_ref[...], _NORM_EPS).astype(_BF16)
        x1.append(x)
        gate.append(_dot(h, wg_ref[...]))
        up.append(_dot(h, wu_ref[...]))
    f = [_dot((g * jax.nn.sigmoid(g) * u).astype(_BF16), wd_ref[...]) for g, u in zip(gate, up)]
    x2, pg, pp = [], [], []
    for r, x, ff in zip(rows, x1, f):
        x = x + _rmsnorm(ff, g3_ref[...], _NORM_EPS)
        x2.append(x)
        pg.append(_dot(x.astype(_BF16), wpg_ref[...]))
        pp.append(_dot(p_ref[r, :].astype(_BF16), wpp_ref[...]))
    for r, x, a, b in zip(rows, x2, pg, pp):
        o_ref[r, :] = x + _rmsnorm(jax.nn.sigmoid(a) * b, g4_ref[...], _NORM_EPS)


def _post(x, lru, att, p, wo_l, wo_a, g1, g2, wg, wu, wd, g3, wpg, wpp, g4, *, tm):
    n, d = x.shape
    tm = min(tm, n)
    d_ff = wg.shape[1]
    row_spec = lambda w: pl.BlockSpec((tm, w), lambda i: (i, 0))
    weights = (wo_l, wo_a, g1, g2, wg, wu, wd, g3, wpg, wpp, g4)
    est = (sum(2 * w.size for w in weights) + 4 * tm * d * 4 + 2 * tm * (lru.shape[1] + att.shape[1]) * 2
           + 2 * tm * p.shape[1] * 4 + 3 * tm * d_ff * 4 + 6 * tm * d * 4)
    return pl.pallas_call(
        _post_kernel,
        grid=(n // tm,),
        in_specs=[row_spec(d), row_spec(lru.shape[1]), row_spec(att.shape[1]), row_spec(p.shape[1])]
                 + [_const_spec(w.shape) for w in weights],
        out_specs=row_spec(d),
        out_shape=jax.ShapeDtypeStruct((n, d), _F32),
        compiler_params=pltpu.CompilerParams(dimension_semantics=("parallel",),
                                             vmem_limit_bytes=_vmem_limit(est)),
        name="post",
    )(x, lru, att, p, *weights)


def _rope_constants(hd):
    rot = hd // _ROT_FRACTION
    half = rot // 2
    inv = _ROPE_THETA ** (-jnp.arange(0, rot, 2, dtype=_F32) / rot)
    c = np.arange(_LANES) % hd
    inv_lane = jnp.where(jnp.asarray(c < rot), inv[c % half], 0.0)
    sign_lo = np.where(c < half, -1.0, 0.0)
    sign_hi = np.where((c >= half) & (c < rot), 1.0, 0.0)
    rows = [inv_lane, jnp.asarray(sign_lo, _F32), jnp.asarray(sign_hi, _F32)]
    rows += [jnp.zeros((_LANES,), _F32)] * (_SUBLANES - len(rows))
    return jnp.stack(rows).astype(_F32), half


def _block_diag(w):
    blocks, n, _ = w.shape
    eye = jnp.eye(blocks, dtype=w.dtype)
    return jnp.einsum('nij,nm->nimj', w, eye).reshape(blocks * n, blocks * n)


def kernel(x_prompt, x_sample, cache_k, cache_v, state_conv, state_h, page_table, p_prompt, p_sample, norm_pre_mix, w_in, conv_w, conv_b, lru_w_a, lru_b_a, lru_w_x, lru_b_x, lru_L, lambda_q1, lambda_k1, lambda_q2, lambda_k2, subln_g, w_out, norm_post_mix, norm_pre_ffn, ffn_w_gate, ffn_w_up, ffn_w_down, norm_post_ffn, ple_w_gate, ple_w_proj, ple_norm):
    depth = w_in.shape[0]
    bsz, seq, d = x_prompt.shape
    n_dec, dec_seq, _ = x_sample.shape
    assert dec_seq == 1
    _, n_phys, page, heads, hw = cache_k.shape
    hd = hw // 2
    att_width = heads * hw
    width = conv_w.shape[-1]
    n_pages = page_table.shape[1]
    past = n_pages * page
    q_scale = hd ** -0.5
    rope_const, half = _rope_constants(hd)
    row = lambda a: a.reshape(1, -1)
    pool_k = cache_k.reshape(depth * n_phys, page, att_width)
    pool_v = cache_v.reshape(depth * n_phys, page, att_width)

    xp = x_prompt
    xs = x_sample.reshape(n_dec, d)
    outs = [[] for _ in range(8)]
    for i in range(depth):
        lam_init = 0.8 - 0.6 * math.exp(-0.3 * i)
        win = w_in[i].astype(_BF16)
        wgate = jnp.concatenate([_block_diag(lru_w_a[i]), _block_diag(lru_w_x[i])], axis=1).astype(_BF16)
        bgate = jnp.concatenate([lru_b_a[i].reshape(1, -1), lru_b_x[i].reshape(1, -1)], axis=1)
        pre_args = (row(norm_pre_mix[i]), win, rope_const, conv_w[i], row(conv_b[i]), wgate, bgate,
                    row(lru_L[i]))
        lam_args = (row(lambda_q1[i]), row(lambda_k1[i]), row(lambda_q2[i]), row(lambda_k2[i]),
                    row(subln_g[i]))
        wo = w_out[i].astype(_BF16)
        post_args = (wo[:width], wo[width:], row(norm_post_mix[i]), row(norm_pre_ffn[i]),
                     ffn_w_gate[i].astype(_BF16), ffn_w_up[i].astype(_BF16), ffn_w_down[i].astype(_BF16),
                     row(norm_post_ffn[i]), ple_w_gate[i].astype(_BF16), ple_w_proj[i].astype(_BF16),
                     row(ple_norm[i]))
        dims = dict(width=width, att_width=att_width, half=half)

        qt, kp, vp, kb, vt, lru, cp, hp = _premix_prompt(xp, *pre_args, tm=256,
                                                         q_scale=q_scale * math.log2(math.e), **dims)
        att = _flash_prompt(qt, kb, vt, *lam_args[:4], subln_g[i].reshape(-1, 1), hd=hd, lam_init=lam_init,
                            t=512)
        xp = _post(xp.reshape(bsz * seq, d), lru.reshape(bsz * seq, width), att.reshape(bsz * seq, att_width),
                   p_prompt[i].reshape(bsz * seq, -1), *post_args, tm=512).reshape(bsz, seq, d)

        qs, ks, vs, lrus, cs, hs = _premix_sample(
            xs, *pre_args, state_conv[i].reshape(n_dec, -1), state_h[i], pos=past, q_scale=q_scale, **dims)
        atts = _decode_sample(page_table, qs, ks, vs, pool_k, pool_v, *lam_args, heads=heads, hd=hd,
                              lam_init=lam_init, page_base=i * n_phys)
        xs = _post(xs, lrus, atts, p_sample[i].reshape(n_dec, -1), *post_args, tm=256)

        for lst, val in zip(outs, (kp.reshape(bsz, seq, heads, hw), vp.reshape(bsz, seq, heads, hw), cp,
                                   hp.reshape(bsz, width), ks.reshape(n_dec, 1, heads, hw),
                                   vs.reshape(n_dec, 1, heads, hw), cs.reshape(n_dec, -1, width), hs)):
            lst.append(val)
    return (xp, xs.reshape(n_dec, 1, d)) + tuple(jnp.stack(o) for o in outs)
```

```python
import functools
import math

import jax
import jax.numpy as jnp
import numpy as np
from jax import lax
from jax.experimental import pallas as pl
from jax.experimental.pallas import tpu as pltpu

_F32 = jnp.float32
_BF16 = jnp.bfloat16

_ROPE_THETA = 500000.0
_ROT_FRACTION = 4
_NEG_INF = -1e30
_NORM_EPS = 1e-6
_SUBLN_EPS = 1e-5
_LRU_LOG_SCALE = 8.0
_POST_GROUP_ROWS = 256

_LANES = 128
_SUBLANES = 8
_MXU_WIDTH = 256
_V7X_VMEM_BYTES = 64 * 1024 * 1024
_VMEM_HEADROOM_BYTES = 8 * 1024 * 1024


def _vmem_limit(estimate_bytes):
    return int(min(max(estimate_bytes, 16 * 1024 * 1024), _V7X_VMEM_BYTES - _VMEM_HEADROOM_BYTES))


def _const_spec(shape):
    return pl.BlockSpec(shape, lambda *_: (0,) * len(shape), pipeline_mode=pl.Buffered(1))


def _dot(a, b):
    return jnp.dot(a, b, preferred_element_type=_F32)


def _dot_nt(a, b):
    return lax.dot_general(a, b, (((1,), (1,)), ((), ())), preferred_element_type=_F32)


def _rmsnorm(x, g, eps):
    ms = jnp.mean(x * x, axis=-1, keepdims=True)
    return x * lax.rsqrt(ms + eps) * g


def _softplus(x):
    return jnp.maximum(x, 0.0) + jnp.log1p(jnp.exp(-jnp.abs(x)))


def _gelu_tanh(x):
    c = math.sqrt(2.0 / math.pi)
    return 0.5 * x * (1.0 + jnp.tanh(c * (x + 0.044715 * (x * x * x))))


def _rope_slab(t, cos, sin_lo, sin_hi, half):
    up = pltpu.roll(t, _LANES - half, axis=1)
    down = pltpu.roll(t, half, axis=1)
    return t * cos + up * sin_lo + down * sin_hi


def _gates_and_coeffs(xc, wgate_ref, bgate_ref, lru_l_ref, width):
    pre = _dot(xc.astype(_BF16), wgate_ref[...]) + bgate_ref[...]
    r = jax.nn.sigmoid(pre[:, :width])
    ig = jax.nn.sigmoid(pre[:, width:])
    log_a = (-_LRU_LOG_SCALE) * r * _softplus(-lru_l_ref[...])
    a = jnp.exp(log_a)
    th = jnp.tanh(log_a)
    var = -2.0 * th / (1.0 - th)
    std = jnp.where(var > 0.0, var * lax.rsqrt(var), 0.0)
    b = std * (ig * xc)
    return a, b


def _premix_prompt_kernel(x_ref, g_ref, win_ref, rope_ref, convw_ref, convb_ref, wgate_ref, bgate_ref,
                          lru_l_ref, qt_ref, k_ref, v_ref, kb_ref, vt_ref, lru_ref, conv_ref, h_ref,
                          xbuf, hcar, hs_buf, cosb, sinb, *, width, att_width, half, q_scale):
    i = pl.program_id(1)
    tm = x_ref.shape[0]
    inv = rope_ref[0:1, :]
    sign_lo = rope_ref[1:2, :]
    sign_hi = rope_ref[2:3, :]

    @pl.when(i == 0)
    def _():
        xbuf[0:_SUBLANES, :] = jnp.zeros((_SUBLANES, width), _F32)
        hcar[...] = jnp.zeros_like(hcar)
        ang = lax.broadcasted_iota(jnp.int32, (tm, _LANES), 0).astype(_F32) * inv
        cosb[...] = jnp.cos(ang)
        sinb[...] = jnp.sin(ang)

    h = _rmsnorm(x_ref[...], g_ref[...], _NORM_EPS).astype(_BF16)
    u = _dot(h, win_ref[...])

    base = (i * tm).astype(_F32) * inv
    cos_a, sin_a = jnp.cos(base), jnp.sin(base)
    cos = cos_a * cosb[...] - sin_a * sinb[...]
    sin = sin_a * cosb[...] + cos_a * sinb[...]
    sin_lo, sin_hi = sin * sign_lo, sin * sign_hi

    q0 = 2 * width
    k0 = q0 + att_width
    v0 = k0 + att_width
    for s in range(att_width // _LANES):
        lo = s * _LANES
        qs = _rope_slab(u[:, q0 + lo:q0 + lo + _LANES], cos, sin_lo, sin_hi, half)
        qt_ref[s] = (qs * q_scale).T.astype(_BF16)
        ks = _rope_slab(u[:, k0 + lo:k0 + lo + _LANES], cos, sin_lo, sin_hi, half)
        k_ref[:, lo:lo + _LANES] = ks
        kb_ref[:, lo:lo + _LANES] = ks.astype(_BF16)
        vt_ref[s] = u[:, v0 + lo:v0 + lo + _LANES].T.astype(_BF16)
    v_ref[...] = u[:, v0:v0 + att_width]

    xl = u[:, :width]
    gl = u[:, width:q0]
    xbuf[_SUBLANES:_SUBLANES + tm, :] = xl
    taps = convw_ref.shape[0]
    xc = convb_ref[...] + xl * convw_ref[taps - 1:taps, :]
    for j in range(1, taps):
        xc = xc + xbuf[pl.ds(_SUBLANES - j, tm), :] * convw_ref[taps - 1 - j:taps - j, :]
    tail = xbuf[pl.ds(tm + _SUBLANES - (taps - 1), taps - 1), :]
    xbuf[_SUBLANES - (taps - 1):_SUBLANES, :] = tail

    a, b = _gates_and_coeffs(xc, wgate_ref, bgate_ref, lru_l_ref, width)
    groups = tm // _SUBLANES
    a3 = a.reshape(groups, _SUBLANES, width)
    b3 = b.reshape(groups, _SUBLANES, width)
    row = lax.broadcasted_iota(jnp.int32, (groups, _SUBLANES, width), 1)
    step = 1
    while step < _SUBLANES:
        keep = row >= step
        b3 = jnp.where(keep, a3 * pltpu.roll(b3, step, axis=1) + b3, b3)
        a3 = jnp.where(keep, a3 * pltpu.roll(a3, step, axis=1), a3)
        step *= 2
    carry = hcar[...]
    for gidx in range(groups):
        hs_buf[gidx * _SUBLANES:(gidx + 1) * _SUBLANES, :] = b3[gidx] + a3[gidx] * carry
        carry = a3[gidx, _SUBLANES - 1:_SUBLANES, :] * carry + b3[gidx, _SUBLANES - 1:_SUBLANES, :]
    hcar[...] = carry
    hs = hs_buf[...]
    lru_ref[...] = (hs * _gelu_tanh(gl)).astype(_BF16)

    @pl.when(i == pl.num_programs(1) - 1)
    def _():
        conv_ref[...] = tail
        h_ref[...] = carry


def _premix_prompt(x, g, win, rope_const, convw, convb, wgate, bgate, lru_l, *, width, att_width, half,
                   q_scale, tm):
    bsz, seq, d = x.shape
    in_width = win.shape[1]
    taps = convw.shape[0]
    row_spec = lambda w: pl.BlockSpec((None, tm, w), lambda b, i: (b, i, 0))
    slabs = att_width // _LANES
    head_t_spec = pl.BlockSpec((None, slabs, _LANES, tm), lambda b, i: (b, 0, 0, i))
    out_shape = (
        jax.ShapeDtypeStruct((bsz, slabs, _LANES, seq), _BF16),
        jax.ShapeDtypeStruct((bsz, seq, att_width), _F32),
        jax.ShapeDtypeStruct((bsz, seq, att_width), _F32),
        jax.ShapeDtypeStruct((bsz, seq, att_width), _BF16),
        jax.ShapeDtypeStruct((bsz, slabs, _LANES, seq), _BF16),
        jax.ShapeDtypeStruct((bsz, seq, width), _BF16),
        jax.ShapeDtypeStruct((bsz, taps - 1, width), _F32),
        jax.ShapeDtypeStruct((bsz, 1, width), _F32),
    )
    est = (2 * win.size + 2 * wgate.size + 2 * tm * d * 4 + 4 * tm * in_width * 4
           + 2 * tm * (3 * att_width * 2 + 2 * att_width * 4 + width * 2) + 6 * tm * width * 4)
    kern = functools.partial(_premix_prompt_kernel, width=width, att_width=att_width, half=half,
                             q_scale=q_scale)
    return pl.pallas_call(
        kern,
        grid=(bsz, seq // tm),
        in_specs=[row_spec(d), _const_spec(g.shape), _const_spec(win.shape), _const_spec(rope_const.shape),
                  _const_spec(convw.shape), _const_spec(convb.shape), _const_spec(wgate.shape),
                  _const_spec(bgate.shape), _const_spec(lru_l.shape)],
        out_specs=(head_t_spec, row_spec(att_width), row_spec(att_width), row_spec(att_width),
                   head_t_spec, row_spec(width),
                   pl.BlockSpec((None, taps - 1, width), lambda b, i: (b, 0, 0)),
                   pl.BlockSpec((None, 1, width), lambda b, i: (b, 0, 0))),
        out_shape=out_shape,
        scratch_shapes=[pltpu.VMEM((tm + _SUBLANES, width), _F32), pltpu.VMEM((1, width), _F32),
                        pltpu.VMEM((tm, width), _F32), pltpu.VMEM((tm, _LANES), _F32),
                        pltpu.VMEM((tm, _LANES), _F32)],
        compiler_params=pltpu.CompilerParams(dimension_semantics=("arbitrary", "arbitrary"),
                                             vmem_limit_bytes=_vmem_limit(est)),
        name="premix_prompt",
    )(x, g, win, rope_const, convw, convb, wgate, bgate, lru_l)


def _premix_sample_kernel(x_ref, g_ref, win_ref, rope_ref, convw_ref, convb_ref, wgate_ref, bgate_ref,
                          lru_l_ref, cst_ref, hst_ref, q_ref, k_ref, v_ref, lru_ref, conv_ref, h_ref,
                          *, width, att_width, half, q_scale, pos):
    inv = rope_ref[0:1, :]
    ang = jnp.float32(pos) * inv
    cos, sin = jnp.cos(ang), jnp.sin(ang)
    sin_lo, sin_hi = sin * rope_ref[1:2, :], sin * rope_ref[2:3, :]

    h = _rmsnorm(x_ref[...], g_ref[...], _NORM_EPS).astype(_BF16)
    u = _dot(h, win_ref[...])
    q0 = 2 * width
    k0 = q0 + att_width
    v0 = k0 + att_width
    for s in range(att_width // _LANES):
        lo = s * _LANES
        qs = _rope_slab(u[:, q0 + lo:q0 + lo + _LANES], cos, sin_lo, sin_hi, half)
        q_ref[:, lo:lo + _LANES] = qs * q_scale
        k_ref[:, lo:lo + _LANES] = _rope_slab(u[:, k0 + lo:k0 + lo + _LANES], cos, sin_lo, sin_hi, half)
    v_ref[...] = u[:, v0:v0 + att_width]

    xl = u[:, :width]
    gl = u[:, width:q0]
    taps = convw_ref.shape[0]
    xc = convb_ref[...] + xl * convw_ref[taps - 1:taps, :]
    for j in range(taps - 1):
        xc = xc + cst_ref[:, j * width:(j + 1) * width] * convw_ref[j:j + 1, :]
    for j in range(taps - 2):
        conv_ref[:, j * width:(j + 1) * width] = cst_ref[:, (j + 1) * width:(j + 2) * width]
    conv_ref[:, (taps - 2) * width:(taps - 1) * width] = xl

    a, b = _gates_and_coeffs(xc, wgate_ref, bgate_ref, lru_l_ref, width)
    hs = a * hst_ref[...] + b
    h_ref[...] = hs
    lru_ref[...] = (hs * _gelu_tanh(gl)).astype(_BF16)


def _premix_sample(x, g, win, rope_const, convw, convb, wgate, bgate, lru_l, conv_state, h_state, *,
                   width, att_width, half, q_scale, pos):
    n, d = x.shape
    taps = convw.shape[0]
    out_shape = (
        jax.ShapeDtypeStruct((n, att_width), _F32),
        jax.ShapeDtypeStruct((n, att_width), _F32),
        jax.ShapeDtypeStruct((n, att_width), _F32),
        jax.ShapeDtypeStruct((n, width), _BF16),
        jax.ShapeDtypeStruct((n, (taps - 1) * width), _F32),
        jax.ShapeDtypeStruct((n, width), _F32),
    )
    args = (x, g, win, rope_const, convw, convb, wgate, bgate, lru_l, conv_state, h_state)
    est = 2 * (2 * win.size + 2 * wgate.size) + 8 * n * win.shape[1] * 4
    kern = functools.partial(_premix_sample_kernel, width=width, att_width=att_width, half=half,
                             q_scale=q_scale, pos=pos)
    return pl.pallas_call(
        kern,
        grid=(1,),
        in_specs=[pl.BlockSpec(a.shape, lambda i, nd=a.ndim: (0,) * nd) for a in args],
        out_specs=tuple(pl.BlockSpec(s.shape, lambda i: (0, 0)) for s in out_shape),
        out_shape=out_shape,
        compiler_params=pltpu.CompilerParams(dimension_semantics=("arbitrary",),
                                             vmem_limit_bytes=_vmem_limit(est)),
        name="premix_sample",
    )(*args)


def _lambda_value(lq1_ref, lk1_ref, lq2_ref, lk2_ref, lam_init):
    d1 = jnp.sum(lq1_ref[...] * lk1_ref[...], axis=-1, keepdims=True)
    d2 = jnp.sum(lq2_ref[...] * lk2_ref[...], axis=-1, keepdims=True)
    return jnp.exp(d1) - jnp.exp(d2) + lam_init


def _flash_kernel(qt_ref, k_ref, vt_ref, lq1_ref, lk1_ref, lq2_ref, lk2_ref, sg_ref, o_ref,
                  qs_ref, s0_ref, s1_ref, m_ref, l_ref, acc_ref, *, lam_init, hd, chunk):
    qi = pl.program_id(2)
    t = qt_ref.shape[1]
    feat = lax.broadcasted_iota(jnp.int32, (_LANES, t), 0)
    qt = qt_ref[...]
    zero = jnp.zeros_like(qt)
    qs_ref[:, :t] = jnp.where(feat < hd, qt, zero)
    qs_ref[:, t:] = jnp.where(feat >= hd, qt, zero)

    m_ref[...] = jnp.full_like(m_ref, _NEG_INF)
    l_ref[...] = jnp.zeros_like(l_ref)
    acc_ref[...] = jnp.zeros_like(acc_ref)

    def scores(j, s_ref):
        kj = k_ref[pl.ds(pl.multiple_of(j * t, t), t), :]
        for c in range(2 * t // chunk):
            cols = slice(c * chunk, (c + 1) * chunk)
            s_ref[:, cols] = _dot(kj, qs_ref[:, cols])

    def process(j, s_ref, masked):
        vtj = vt_ref[:, pl.ds(pl.multiple_of(j * t, t), t)]
        for c in range(2 * t // chunk):
            cols = slice(c * chunk, (c + 1) * chunk)
            s = s_ref[:, cols]
            if masked:
                kk = lax.broadcasted_iota(jnp.int32, (t, chunk), 0)
                qq = lax.broadcasted_iota(jnp.int32, (t, chunk), 1) + (c * chunk) % t
                s = jnp.where(kk <= qq, s, _NEG_INF)
            m_prev = m_ref[:, cols]
            m_next = jnp.maximum(m_prev, jnp.max(s, axis=0, keepdims=True))
            p = jnp.exp2(s - m_next)
            alpha = jnp.exp2(m_prev - m_next)
            l_ref[:, cols] = alpha * l_ref[:, cols] + jnp.sum(p, axis=0, keepdims=True)
            acc_ref[:, cols] = alpha * acc_ref[:, cols] + _dot(vtj, p.astype(_BF16))
            m_ref[:, cols] = m_next

    scores(0, s0_ref)

    def body(jj, carry):
        j = 2 * jj
        scores(j + 1, s1_ref)
        process(j, s0_ref, False)
        scores(j + 2, s0_ref)
        process(j + 1, s1_ref, False)
        return carry

    lax.fori_loop(0, qi // 2, body, 0)

    @pl.when(qi % 2 == 0)
    def _():
        process(qi, s0_ref, True)

    @pl.when(qi % 2 == 1)
    def _():
        scores(qi, s1_ref)
        process(qi - 1, s0_ref, False)
        process(qi, s1_ref, True)

    lam = _lambda_value(lq1_ref, lk1_ref, lq2_ref, lk2_ref, lam_init)
    o_all = acc_ref[...] * (1.0 / l_ref[...])
    o = o_all[:, :t] - lam * o_all[:, t:]
    ms = jnp.mean(o * o, axis=0, keepdims=True)
    o = o * lax.rsqrt(ms + _SUBLN_EPS) * sg_ref[...] * (1.0 - lam_init)
    o_ref[...] = o.T.astype(o_ref.dtype)


def _flash_prompt(qt, k, vt, lq1, lk1, lq2, lk2, sg_col, *, hd, lam_init, t):
    bsz, heads, hw, seq = qt.shape
    assert hw == _LANES and 2 * hd == hw
    qt_spec = pl.BlockSpec((None, None, hw, t), lambda b, h, i: (b, h, 0, i))
    k_spec = pl.BlockSpec((None, seq, hw), lambda b, h, i: (b, 0, h))
    vt_spec = pl.BlockSpec((None, None, hw, seq), lambda b, h, i: (b, h, 0, 0))
    o_spec = pl.BlockSpec((None, t, hw), lambda b, h, i: (b, i, h))
    small = [_const_spec(a.shape) for a in (lq1, lk1, lq2, lk2, sg_col)]
    est = 2 * 2 * seq * hw * 2 + 8 * t * hw * 4 + 6 * (2 * t) * t * 4 + 2 * hw * 2 * t * 4
    kern = functools.partial(_flash_kernel, lam_init=lam_init, hd=hd, chunk=_MXU_WIDTH)
    return pl.pallas_call(
        kern,
        grid=(bsz, heads, seq // t),
        in_specs=[qt_spec, k_spec, vt_spec] + small,
        out_specs=o_spec,
        out_shape=jax.ShapeDtypeStruct((bsz, seq, heads * hw), _BF16),
        scratch_shapes=[pltpu.VMEM((hw, 2 * t), _BF16), pltpu.VMEM((t, 2 * t), _F32),
                        pltpu.VMEM((t, 2 * t), _F32), pltpu.VMEM((1, 2 * t), _F32),
                        pltpu.VMEM((1, 2 * t), _F32), pltpu.VMEM((hw, 2 * t), _F32)],
        compiler_params=pltpu.CompilerParams(dimension_semantics=("parallel", "parallel", "arbitrary"),
                                             vmem_limit_bytes=_vmem_limit(est)),
        name="flash_prompt",
    )(qt, k, vt, lq1, lk1, lq2, lk2, sg_col)


def _decode_kernel(pt_ref, q_ref, kn_ref, vn_ref, lq1_ref, lk1_ref, lq2_ref, lk2_ref, sg_ref, *rest,
                   lam_init, hd, heads, n_pages):
    del pt_ref
    k_refs = rest[:n_pages]
    v_refs = rest[n_pages:2 * n_pages]
    o_ref = rest[2 * n_pages]
    maps, hw = q_ref.shape
    page_rows = k_refs[0].shape[0]
    r = lax.broadcasted_iota(jnp.int32, (maps, hw), 0)
    c = lax.broadcasted_iota(jnp.int32, (maps, hw), 1)
    qmat = jnp.where(((r & 1) == 0) == (c < hd), q_ref[...], 0.0)
    qmat_b = qmat.astype(_BF16)
    row_head = lax.broadcasted_iota(jnp.int32, (maps, page_rows), 0) >> 1
    col_head = lax.broadcasted_iota(jnp.int32, (maps, page_rows), 1) & (heads - 1)
    same_head = row_head == col_head

    scores = [jnp.where(same_head, _dot_nt(qmat_b, kr[...].astype(_BF16)), _NEG_INF)
              for kr in k_refs]
    s_new = jnp.sum(qmat * kn_ref[...], axis=1, keepdims=True)
    m = s_new
    for s in scores:
        m = jnp.maximum(m, jnp.max(s, axis=1, keepdims=True))
    p_new = jnp.exp(s_new - m)
    l = p_new
    acc = p_new * vn_ref[...]
    for s, vr in zip(scores, v_refs):
        p = jnp.exp(s - m)
        l = l + jnp.sum(p, axis=1, keepdims=True)
        acc = acc + _dot(p.astype(_BF16), vr[...].astype(_BF16))
    o_all = acc / l

    lam = _lambda_value(lq1_ref, lk1_ref, lq2_ref, lk2_ref, lam_init)
    o = o_all - lam * pltpu.roll(o_all, maps - 1, axis=0)
    o_ref[...] = _rmsnorm(o, sg_ref[...], _SUBLN_EPS) * (1.0 - lam_init)


def _decode_sample(page_table, q, k_new, v_new, pool_k, pool_v, lq1, lk1, lq2, lk2, sg, *, heads, hd,
                   lam_init, page_base):
    n, att_width = q.shape
    n_pages = page_table.shape[1]
    _, page_rows, hw = pool_k.shape
    assert heads & (heads - 1) == 0 and hw == 2 * hd
    maps = 2 * heads
    per_map = lambda a: jnp.repeat(a.reshape(n, heads, hw), 2, axis=1)
    row_spec = pl.BlockSpec((None, maps, hw), lambda b, pt: (b, 0, 0))
    small = [pl.BlockSpec(a.shape, lambda b, pt: (0, 0)) for a in (lq1, lk1, lq2, lk2, sg)]
    page_specs = [pl.BlockSpec((None, page_rows, hw), lambda b, pt, j=j: (pt[b, j] + page_base, 0, 0))
                  for j in range(n_pages)]
    est = 2 * 2 * n_pages * page_rows * hw * 4 + 8 * n_pages * page_rows * hw * 2
    kern = functools.partial(_decode_kernel, lam_init=lam_init, hd=hd, heads=heads, n_pages=n_pages)
    out = pl.pallas_call(
        kern,
        grid_spec=pltpu.PrefetchScalarGridSpec(
            num_scalar_prefetch=1,
            grid=(n,),
            in_specs=[row_spec, row_spec, row_spec] + small + page_specs + page_specs,
            out_specs=row_spec,
        ),
        out_shape=jax.ShapeDtypeStruct((n, maps, hw), _F32),
        compiler_params=pltpu.CompilerParams(dimension_semantics=("arbitrary",),
                                             vmem_limit_bytes=_vmem_limit(est)),
        name="decode_sample",
    )(page_table, per_map(q), per_map(k_new), per_map(v_new), lq1, lk1, lq2, lk2, sg,
      *([pool_k] * n_pages), *([pool_v] * n_pages))
    return out[:, 0::2, :].reshape(n, att_width).astype(_BF16)


def _post_kernel(x_ref, lru_ref, att_ref, p_ref, wo_l_ref, wo_a_ref, g1_ref, g2_ref, wg_ref, wu_ref,
                 wd_ref, g3_ref, wpg_ref, wpp_ref, g4_ref, o_ref):
    tm = x_ref.shape[0]
    n_groups = max(1, tm // _POST_GROUP_ROWS)
    rows = [slice(r * (tm // n_groups), (r + 1) * (tm // n_groups)) for r in range(n_groups)]
    mix = [_dot(lru_ref[r, :], wo_l_ref[...]) + _dot(att_ref[r, :], wo_a_ref[...]) for r in rows]
    x1, gate, up = [], [], []
    for r, mx in zip(rows, mix):
        x = x_ref[r, :] + _rmsnorm(mx, g1_ref[...], _NORM_EPS)
        h = _rmsnorm(x, g2_ref[...], _NORM_EPS).astype(_BF16)
        x1.append(x)
        gate.append(_dot(h, wg_ref[...]))
        up.append(_dot(h, wu_ref[...]))
    f = [_dot((g * jax.nn.sigmoid(g) * u).astype(_BF16), wd_ref[...]) for g, u in zip(gate, up)]
    x2, pg, pp = [], [], []
    for r, x, ff in zip(rows, x1, f):
        x = x + _rmsnorm(ff, g3_ref[...], _NORM_EPS)
        x2.append(x)
        pg.append(_dot(x.astype(_BF16), wpg_ref[...]))
        pp.append(_dot(p_ref[r, :].astype(_BF16), wpp_ref[...]))
    for r, x, a, b in zip(rows, x2, pg, pp):
        o_ref[r, :] = x + _rmsnorm(jax.nn.sigmoid(a) * b, g4_ref[...], _NORM_EPS)


def _post(x, lru, att, p, wo_l, wo_a, g1, g2, wg, wu, wd, g3, wpg, wpp, g4, *, tm):
    n, d = x.shape
    tm = min(tm, n)
    d_ff = wg.shape[1]
    row_spec = lambda w: pl.BlockSpec((tm, w), lambda i: (i, 0))
    weights = (wo_l, wo_a, g1, g2, wg, wu, wd, g3, wpg, wpp, g4)
    est = (sum(2 * w.size for w in weights) + 4 * tm * d * 4 + 2 * tm * (lru.shape[1] + att.shape[1]) * 2
           + 2 * tm * p.shape[1] * 4 + 3 * tm * d_ff * 4 + 6 * tm * d * 4)
    return pl.pallas_call(
        _post_kernel,
        grid=(n // tm,),
        in_specs=[row_spec(d), row_spec(lru.shape[1]), row_spec(att.shape[1]), row_spec(p.shape[1])]
                 + [_const_spec(w.shape) for w in weights],
        out_specs=row_spec(d),
        out_shape=jax.ShapeDtypeStruct((n, d), _F32),
        compiler_params=pltpu.CompilerParams(dimension_semantics=("parallel",),
                                             vmem_limit_bytes=_vmem_limit(est)),
        name="post",
    )(x, lru, att, p, *weights)


def _rope_constants(hd):
    rot = hd // _ROT_FRACTION
    half = rot // 2
    inv = _ROPE_THETA ** (-jnp.arange(0, rot, 2, dtype=_F32) / rot)
    c = np.arange(_LANES) % hd
    inv_lane = jnp.where(jnp.asarray(c < rot), inv[c % half], 0.0)
    sign_lo = np.where(c < half, -1.0, 0.0)
    sign_hi = np.where((c >= half) & (c < rot), 1.0, 0.0)
    rows = [inv_lane, jnp.asarray(sign_lo, _F32), jnp.asarray(sign_hi, _F32)]
    rows += [jnp.zeros((_LANES,), _F32)] * (_SUBLANES - len(rows))
    return jnp.stack(rows).astype(_F32), half


def _block_diag(w):
    blocks, n, _ = w.shape
    eye = jnp.eye(blocks, dtype=w.dtype)
    return jnp.einsum('nij,nm->nimj', w, eye).reshape(blocks * n, blocks * n)


def kernel(x_prompt, x_sample, cache_k, cache_v, state_conv, state_h, page_table, p_prompt, p_sample, norm_pre_mix, w_in, conv_w, conv_b, lru_w_a, lru_b_a, lru_w_x, lru_b_x, lru_L, lambda_q1, lambda_k1, lambda_q2, lambda_k2, subln_g, w_out, norm_post_mix, norm_pre_ffn, ffn_w_gate, ffn_w_up, ffn_w_down, norm_post_ffn, ple_w_gate, ple_w_proj, ple_norm):
    depth = w_in.shape[0]
    bsz, seq, d = x_prompt.shape
    n_dec, dec_seq, _ = x_sample.shape
    assert dec_seq == 1
    _, n_phys, page, heads, hw = cache_k.shape
    hd = hw // 2
    att_width = heads * hw
    width = conv_w.shape[-1]
    n_pages = page_table.shape[1]
    past = n_pages * page
    q_scale = hd ** -0.5
    rope_const, half = _rope_constants(hd)
    row = lambda a: a.reshape(1, -1)
    pool_k = cache_k.reshape(depth * n_phys, page * heads, hw)
    pool_v = cache_v.reshape(depth * n_phys, page * heads, hw)

    xp = x_prompt
    xs = x_sample.reshape(n_dec, d)
    outs = [[] for _ in range(8)]
    for i in range(depth):
        lam_init = 0.8 - 0.6 * math.exp(-0.3 * i)
        win = w_in[i].astype(_BF16)
        wgate = jnp.concatenate([_block_diag(lru_w_a[i]), _block_diag(lru_w_x[i])], axis=1).astype(_BF16)
        bgate = jnp.concatenate([lru_b_a[i].reshape(1, -1), lru_b_x[i].reshape(1, -1)], axis=1)
        pre_args = (row(norm_pre_mix[i]), win, rope_const, conv_w[i], row(conv_b[i]), wgate, bgate,
                    row(lru_L[i]))
        lam_args = (row(lambda_q1[i]), row(lambda_k1[i]), row(lambda_q2[i]), row(lambda_k2[i]),
                    row(subln_g[i]))
        wo = w_out[i].astype(_BF16)
        post_args = (wo[:width], wo[width:], row(norm_post_mix[i]), row(norm_pre_ffn[i]),
                     ffn_w_gate[i].astype(_BF16), ffn_w_up[i].astype(_BF16), ffn_w_down[i].astype(_BF16),
                     row(norm_post_ffn[i]), ple_w_gate[i].astype(_BF16), ple_w_proj[i].astype(_BF16),
                     row(ple_norm[i]))
        dims = dict(width=width, att_width=att_width, half=half)

        qt, kp, vp, kb, vt, lru, cp, hp = _premix_prompt(xp, *pre_args, tm=256,
                                                         q_scale=q_scale * math.log2(math.e), **dims)
        att = _flash_prompt(qt, kb, vt, *lam_args[:4], subln_g[i].reshape(-1, 1), hd=hd, lam_init=lam_init,
                            t=512)
        xp = _post(xp.reshape(bsz * seq, d), lru.reshape(bsz * seq, width), att.reshape(bsz * seq, att_width),
                   p_prompt[i].reshape(bsz * seq, -1), *post_args, tm=512).reshape(bsz, seq, d)

        qs, ks, vs, lrus, cs, hs = _premix_sample(
            xs, *pre_args, state_conv[i].reshape(n_dec, -1), state_h[i], pos=past, q_scale=q_scale, **dims)
        atts = _decode_sample(page_table, qs, ks, vs, pool_k, pool_v, *lam_args, heads=heads, hd=hd,
                              lam_init=lam_init, page_base=i * n_phys)
        xs = _post(xs, lrus, atts, p_sample[i].reshape(n_dec, -1), *post_args, tm=256)

        for lst, val in zip(outs, (kp.reshape(bsz, seq, heads, hw), vp.reshape(bsz, seq, heads, hw), cp,
                                   hp.reshape(bsz, width), ks.reshape(n_dec, 1, heads, hw),
                                   vs.reshape(n_dec, 1, heads, hw), cs.reshape(n_dec, -1, width), hs)):
            lst.append(val)
    return (xp, xs.reshape(n_dec, 1, d)) + tuple(jnp.stack(o) for o in outs)
```

```python
import functools
import math

import jax
import jax.numpy as jnp
import numpy as np
from jax import lax
from jax.experimental import pallas as pl
from jax.experimental.pallas import tpu as pltpu

_F32 = jnp.float32
_BF16 = jnp.bfloat16

_ROPE_THETA = 500000.0
_ROT_FRACTION = 4
_NEG_INF = -1e30
_NORM_EPS = 1e-6
_SUBLN_EPS = 1e-5
_LRU_LOG_SCALE = 8.0
_POST_GROUP_ROWS = 256

_LANES = 128
_SUBLANES = 8
_MXU_WIDTH = 256
_V7X_VMEM_BYTES = 64 * 1024 * 1024
_VMEM_HEADROOM_BYTES = 8 * 1024 * 1024


def _vmem_limit(estimate_bytes):
    return int(min(max(estimate_bytes, 16 * 1024 * 1024), _V7X_VMEM_BYTES - _VMEM_HEADROOM_BYTES))


def _const_spec(shape):
    return pl.BlockSpec(shape, lambda *_: (0,) * len(shape), pipeline_mode=pl.Buffered(1))


def _dot(a, b):
    return jnp.dot(a, b, preferred_element_type=_F32)


def _dot_nt(a, b):
    return lax.dot_general(a, b, (((1,), (1,)), ((), ())), preferred_element_type=_F32)


def _rmsnorm(x, g, eps):
    ms = jnp.mean(x * x, axis=-1, keepdims=True)
    return x * lax.rsqrt(ms + eps) * g


def _softplus(x):
    return jnp.maximum(x, 0.0) + jnp.log1p(jnp.exp(-jnp.abs(x)))


def _gelu_tanh(x):
    c = math.sqrt(2.0 / math.pi)
    return 0.5 * x * (1.0 + jnp.tanh(c * (x + 0.044715 * (x * x * x))))


def _rope_slab(t, cos, sin_lo, sin_hi, half):
    up = pltpu.roll(t, _LANES - half, axis=1)
    down = pltpu.roll(t, half, axis=1)
    return t * cos + up * sin_lo + down * sin_hi


def _gates_and_coeffs(xc, wgate_ref, bgate_ref, lru_l_ref, width):
    pre = _dot(xc.astype(_BF16), wgate_ref[...]) + bgate_ref[...]
    r = jax.nn.sigmoid(pre[:, :width])
    ig = jax.nn.sigmoid(pre[:, width:])
    log_a = (-_LRU_LOG_SCALE) * r * _softplus(-lru_l_ref[...])
    a = jnp.exp(log_a)
    th = jnp.tanh(log_a)
    var = -2.0 * th / (1.0 - th)
    std = jnp.where(var > 0.0, var * lax.rsqrt(var), 0.0)
    b = std * (ig * xc)
    return a, b


def _premix_prompt_kernel(x_ref, g_ref, win_ref, rope_ref, convw_ref, convb_ref, wgate_ref, bgate_ref,
                          lru_l_ref, qt_ref, k_ref, v_ref, kb_ref, vt_ref, lru_ref, conv_ref, h_ref,
                          xbuf, hcar, hs_buf, cosb, sinb, *, width, att_width, half, q_scale):
    i = pl.program_id(1)
    tm = x_ref.shape[0]
    inv = rope_ref[0:1, :]
    sign_lo = rope_ref[1:2, :]
    sign_hi = rope_ref[2:3, :]

    @pl.when(i == 0)
    def _():
        xbuf[0:_SUBLANES, :] = jnp.zeros((_SUBLANES, width), _F32)
        hcar[...] = jnp.zeros_like(hcar)
        ang = lax.broadcasted_iota(jnp.int32, (tm, _LANES), 0).astype(_F32) * inv
        cosb[...] = jnp.cos(ang)
        sinb[...] = jnp.sin(ang)

    h = _rmsnorm(x_ref[...], g_ref[...], _NORM_EPS).astype(_BF16)
    u = _dot(h, win_ref[...])

    base = (i * tm).astype(_F32) * inv
    cos_a, sin_a = jnp.cos(base), jnp.sin(base)
    cos = cos_a * cosb[...] - sin_a * sinb[...]
    sin = sin_a * cosb[...] + cos_a * sinb[...]
    sin_lo, sin_hi = sin * sign_lo, sin * sign_hi

    q0 = 2 * width
    k0 = q0 + att_width
    v0 = k0 + att_width
    slabs = att_width // _LANES
    for s in range(slabs):
        lo = s * _LANES
        qs = _rope_slab(u[:, q0 + lo:q0 + lo + _LANES], cos, sin_lo, sin_hi, half)
        qt_ref[s] =(qs * q_scale).T.astype(_BF16)
        ks = _rope_slab(u[:, k0 + lo:k0 + lo + _LANES], cos, sin_lo, sin_hi, half)
        vs = u[:, v0 + lo:v0 + lo + _LANES]
        k_ref[pl.ds(s, tm, stride=slabs), :] = ks
        v_ref[pl.ds(s, tm, stride=slabs), :] = vs
        kb_ref[:, lo:lo + _LANES] = ks.astype(_BF16)
        vt_ref[s] = vs.T.astype(_BF16)

    xl = u[:, :width]
    gl = u[:, width:q0]
    xbuf[_SUBLANES:_SUBLANES + tm, :] = xl
    taps = convw_ref.shape[0]
    xc = convb_ref[...] + xl * convw_ref[taps - 1:taps, :]
    for j in range(1, taps):
        xc = xc + xbuf[pl.ds(_SUBLANES - j, tm), :] * convw_ref[taps - 1 - j:taps - j, :]
    tail = xbuf[pl.ds(tm + _SUBLANES - (taps - 1), taps - 1), :]
    xbuf[_SUBLANES - (taps - 1):_SUBLANES, :] = tail

    a, b = _gates_and_coeffs(xc, wgate_ref, bgate_ref, lru_l_ref, width)
    groups = tm // _SUBLANES
    a3 = a.reshape(groups, _SUBLANES, width)
    b3 = b.reshape(groups, _SUBLANES, width)
    row = lax.broadcasted_iota(jnp.int32, (groups, _SUBLANES, width), 1)
    step = 1
    while step < _SUBLANES:
        keep = row >= step
        b3 = jnp.where(keep, a3 * pltpu.roll(b3, step, axis=1) + b3, b3)
        a3 = jnp.where(keep, a3 * pltpu.roll(a3, step, axis=1), a3)
        step *= 2
    carry = hcar[...]
    for gidx in range(groups):
        hs_buf[gidx * _SUBLANES:(gidx + 1) * _SUBLANES, :] = b3[gidx] + a3[gidx] * carry
        carry = a3[gidx, _SUBLANES - 1:_SUBLANES, :] * carry + b3[gidx, _SUBLANES - 1:_SUBLANES, :]
    hcar[...] = carry
    hs = hs_buf[...]
    lru_ref[...] = (hs * _gelu_tanh(gl)).astype(_BF16)

    @pl.when(i == pl.num_programs(1) - 1)
    def _():
        conv_ref[...] = tail
        h_ref[...] = carry


def _premix_prompt(x, g, win, rope_const, convw, convb, wgate, bgate, lru_l, *, width, att_width, half,
                   q_scale, tm):
    bsz, seq, d = x.shape
    in_width = win.shape[1]
    taps = convw.shape[0]
    row_spec = lambda w: pl.BlockSpec((None, tm, w), lambda b, i: (b, i, 0))
    slabs = att_width // _LANES
    head_t_spec = pl.BlockSpec((None, slabs, _LANES, tm), lambda b, i: (b, 0, 0, i))
    out_shape = (
        jax.ShapeDtypeStruct((bsz, slabs, _LANES, seq), _BF16),
        jax.ShapeDtypeStruct((bsz, seq * slabs, _LANES), _F32),
        jax.ShapeDtypeStruct((bsz, seq * slabs, _LANES), _F32),
        jax.ShapeDtypeStruct((bsz, seq, att_width), _BF16),
        jax.ShapeDtypeStruct((bsz, slabs, _LANES, seq), _BF16),
        jax.ShapeDtypeStruct((bsz, seq, width), _BF16),
        jax.ShapeDtypeStruct((bsz, taps - 1, width), _F32),
        jax.ShapeDtypeStruct((bsz, 1, width), _F32),
    )
    est = (2 * win.size + 2 * wgate.size + 2 * tm * d * 4 + 4 * tm * in_width * 4
           + 2 * tm * (3 * att_width * 2 + 2 * att_width * 4 + width * 2) + 6 * tm * width * 4)
    kern = functools.partial(_premix_prompt_kernel, width=width, att_width=att_width, half=half,
                             q_scale=q_scale)
    return pl.pallas_call(
        kern,
        grid=(bsz, seq // tm),
        in_specs=[row_spec(d), _const_spec(g.shape), _const_spec(win.shape), _const_spec(rope_const.shape),
                  _const_spec(convw.shape), _const_spec(convb.shape), _const_spec(wgate.shape),
                  _const_spec(bgate.shape), _const_spec(lru_l.shape)],
        out_specs=(head_t_spec, pl.BlockSpec((None, tm * slabs, _LANES), lambda b, i: (b, i, 0)),
                   pl.BlockSpec((None, tm * slabs, _LANES), lambda b, i: (b, i, 0)), row_spec(att_width),
                   head_t_spec, row_spec(width),
                   pl.BlockSpec((None, taps - 1, width), lambda b, i: (b, 0, 0)),
                   pl.BlockSpec((None, 1, width), lambda b, i: (b, 0, 0))),
        out_shape=out_shape,
        scratch_shapes=[pltpu.VMEM((tm + _SUBLANES, width), _F32), pltpu.VMEM((1, width), _F32),
                        pltpu.VMEM((tm, width), _F32), pltpu.VMEM((tm, _LANES), _F32),
                        pltpu.VMEM((tm, _LANES), _F32)],
        compiler_params=pltpu.CompilerParams(dimension_semantics=("arbitrary", "arbitrary"),
                                             vmem_limit_bytes=_vmem_limit(est)),
        name="premix_prompt",
    )(x, g, win, rope_const, convw, convb, wgate, bgate, lru_l)


def _premix_sample_kernel(x_ref, g_ref, win_ref, rope_ref, convw_ref, convb_ref, wgate_ref, bgate_ref,
                          lru_l_ref, cst_ref, hst_ref, q_ref, k_ref, v_ref, lru_ref, conv_ref, h_ref,
                          *, width, att_width, half, q_scale, pos):
    inv = rope_ref[0:1, :]
    ang = jnp.float32(pos) * inv
    cos, sin = jnp.cos(ang), jnp.sin(ang)
    sin_lo, sin_hi = sin * rope_ref[1:2, :], sin * rope_ref[2:3, :]

    h = _rmsnorm(x_ref[...], g_ref[...], _NORM_EPS).astype(_BF16)
    u = _dot(h, win_ref[...])
    q0 = 2 * width
    k0 = q0 + att_width
    v0 = k0 + att_width
    for s in range(att_width // _LANES):
        lo = s * _LANES
        qs = _rope_slab(u[:, q0 + lo:q0 + lo + _LANES], cos, sin_lo, sin_hi, half)
        q_ref[:, lo:lo + _LANES] = qs * q_scale
        k_ref[:, lo:lo + _LANES] = _rope_slab(u[:, k0 + lo:k0 + lo + _LANES], cos, sin_lo, sin_hi, half)
    v_ref[...] = u[:, v0:v0 + att_width]

    xl = u[:, :width]
    gl = u[:, width:q0]
    taps = convw_ref.shape[0]
    xc = convb_ref[...] + xl * convw_ref[taps - 1:taps, :]
    for j in range(taps - 1):
        xc = xc + cst_ref[:, j * width:(j + 1) * width] * convw_ref[j:j + 1, :]
    for j in range(taps - 2):
        conv_ref[:, j * width:(j + 1) * width] = cst_ref[:, (j + 1) * width:(j + 2) * width]
    conv_ref[:, (taps - 2) * width:(taps - 1) * width] = xl

    a, b = _gates_and_coeffs(xc, wgate_ref, bgate_ref, lru_l_ref, width)
    hs = a * hst_ref[...] + b
    h_ref[...] = hs
    lru_ref[...] = (hs * _gelu_tanh(gl)).astype(_BF16)


def _premix_sample(x, g, win, rope_const, convw, convb, wgate, bgate, lru_l, conv_state, h_state, *,
                   width, att_width, half, q_scale, pos):
    n, d = x.shape
    taps = convw.shape[0]
    out_shape = (
        jax.ShapeDtypeStruct((n, att_width), _F32),
        jax.ShapeDtypeStruct((n, att_width), _F32),
        jax.ShapeDtypeStruct((n, att_width), _F32),
        jax.ShapeDtypeStruct((n, width), _BF16),
        jax.ShapeDtypeStruct((n, (taps - 1) * width), _F32),
        jax.ShapeDtypeStruct((n, width), _F32),
    )
    args = (x, g, win, rope_const, convw, convb, wgate, bgate, lru_l, conv_state, h_state)
    est = 2 * (2 * win.size + 2 * wgate.size) + 8 * n * win.shape[1] * 4
    kern = functools.partial(_premix_sample_kernel, width=width, att_width=att_width, half=half,
                             q_scale=q_scale, pos=pos)
    return pl.pallas_call(
        kern,
        grid=(1,),
        in_specs=[pl.BlockSpec(a.shape, lambda i, nd=a.ndim: (0,) * nd) for a in args],
        out_specs=tuple(pl.BlockSpec(s.shape, lambda i: (0, 0)) for s in out_shape),
        out_shape=out_shape,
        compiler_params=pltpu.CompilerParams(dimension_semantics=("arbitrary",),
                                             vmem_limit_bytes=_vmem_limit(est)),
        name="premix_sample",
    )(*args)


def _lambda_value(lq1_ref, lk1_ref, lq2_ref, lk2_ref, lam_init):
    d1 = jnp.sum(lq1_ref[...] * lk1_ref[...], axis=-1, keepdims=True)
    d2 = jnp.sum(lq2_ref[...] * lk2_ref[...], axis=-1, keepdims=True)
    return jnp.exp(d1) - jnp.exp(d2) + lam_init


def _flash_kernel(qt_ref, k_ref, vt_ref, lq1_ref, lk1_ref, lq2_ref, lk2_ref, sg_ref, o_ref,
                  qs_ref, s0_ref, s1_ref, bm0_ref, bm1_ref, m_ref, l_ref, acc_ref, *, lam_init, hd, chunk):
    qi = pl.program_id(2)
    tq = qt_ref.shape[1]
    tk = s0_ref.shape[0]
    feat = lax.broadcasted_iota(jnp.int32, (_LANES, tq), 0)
    qt = qt_ref[...]
    zero = jnp.zeros_like(qt)
    qs_ref[:, :tq] = jnp.where(feat < hd, qt, zero)
    qs_ref[:, tq:] = jnp.where(feat >= hd, qt, zero)

    m_ref[...] = jnp.full_like(m_ref, _NEG_INF)
    l_ref[...] = jnp.zeros_like(l_ref)
    acc_ref[...] = jnp.zeros_like(acc_ref)

    n_chunks = 2 * tq // chunk
    offsets = [(c * chunk) % tq for c in range(n_chunks)]

    def visible(off, r):
        return r is None or off + chunk - 1 >= r * tk

    def fully_visible(off, r):
        return r is None or off >= r * tk + tk - 1

    def scores(j, s_ref, bm_ref, r=None):
        kj = k_ref[pl.ds(pl.multiple_of(j * tk, tk), tk), :]
        for c, off in enumerate(offsets):
            if not visible(off, r):
                continue
            cols = slice(c * chunk, (c + 1) * chunk)
            s = _dot(kj, qs_ref[:, cols])
            s_ref[:, cols] = s
            bm_ref[:, cols] = jnp.max(s, axis=0, keepdims=True)

    def process(j, s_ref, bm_ref, r=None):
        vtj = vt_ref[:, pl.ds(pl.multiple_of(j * tk, tk), tk)]
        for c, off in enumerate(offsets):
            if not visible(off, r):
                continue
            cols = slice(c * chunk, (c + 1) * chunk)
            s = s_ref[:, cols]
            if fully_visible(off, r):
                block_max = bm_ref[:, cols]
            else:
                kk = lax.broadcasted_iota(jnp.int32, (tk, chunk), 0) + r * tk
                qq = lax.broadcasted_iota(jnp.int32, (tk, chunk), 1) + off
                s = jnp.where(kk <= qq, s, _NEG_INF)
                block_max = jnp.max(s, axis=0, keepdims=True)
            m_prev = m_ref[:, cols]
            m_next = jnp.maximum(m_prev, block_max)
            p = jnp.exp2(s - m_next)
            alpha = jnp.exp2(m_prev - m_next)
            l_ref[:, cols] = alpha * l_ref[:, cols] + jnp.sum(p, axis=0, keepdims=True)
            acc_ref[:, cols] = alpha * acc_ref[:, cols] + _dot(vtj, p.astype(_BF16))
            m_ref[:, cols] = m_next

    scores(0, s0_ref, bm0_ref)

    def body(jj, carry):
        j = 2 * jj
        scores(j + 1, s1_ref, bm1_ref)
        process(j, s0_ref, bm0_ref)
        scores(j + 2, s0_ref, bm0_ref)
        process(j + 1, s1_ref, bm1_ref)
        return carry

    lax.fori_loop(0, qi, body, 0)
    scores(2 * qi + 1, s1_ref, bm1_ref, 1)
    process(2 * qi, s0_ref, bm0_ref, 0)
    process(2 * qi + 1, s1_ref, bm1_ref, 1)

    lam = _lambda_value(lq1_ref, lk1_ref, lq2_ref, lk2_ref, lam_init)
    o_all = acc_ref[...] * (1.0 / l_ref[...])
    o = o_all[:, :tq] - lam * o_all[:, tq:]
    ms = jnp.mean(o * o, axis=0, keepdims=True)
    o = o * lax.rsqrt(ms + _SUBLN_EPS) * sg_ref[...] * (1.0 - lam_init)
    o_ref[...] = o.T.astype(o_ref.dtype)


def _flash_prompt(qt, k, vt, lq1, lk1, lq2, lk2, sg_col, *, hd, lam_init, tk):
    bsz, heads, hw, seq = qt.shape
    assert hw == _LANES and 2 * hd == hw
    t = 2 * tk
    qt_spec = pl.BlockSpec((None, None, hw, t), lambda b, h, i: (b, h, 0, i))
    k_spec = pl.BlockSpec((None, seq, hw), lambda b, h, i: (b, 0, h))
    vt_spec = pl.BlockSpec((None, None, hw, seq), lambda b, h, i: (b, h, 0, 0))
    o_spec = pl.BlockSpec((None, t, hw), lambda b, h, i: (b, i, h))
    small = [_const_spec(a.shape) for a in (lq1, lk1, lq2, lk2, sg_col)]
    est = 2 * 2 * seq * hw * 2 + 8 * t * hw * 4 + 6 * (2 * t) * tk * 4 + 2 * hw * 2 * t * 4
    kern = functools.partial(_flash_kernel, lam_init=lam_init, hd=hd, chunk=_MXU_WIDTH)
    return pl.pallas_call(
        kern,
        grid=(bsz, heads, seq // t),
        in_specs=[qt_spec, k_spec, vt_spec] + small,
        out_specs=o_spec,
        out_shape=jax.ShapeDtypeStruct((bsz, seq, heads * hw), _BF16),
        scratch_shapes=[pltpu.VMEM((hw, 2 * t), _BF16), pltpu.VMEM((tk, 2 * t), _F32),
                        pltpu.VMEM((tk, 2 * t), _F32), pltpu.VMEM((1, 2 * t), _F32),
                        pltpu.VMEM((1, 2 * t), _F32), pltpu.VMEM((1, 2 * t), _F32),
                        pltpu.VMEM((1, 2 * t), _F32), pltpu.VMEM((hw, 2 * t), _F32)],
        compiler_params=pltpu.CompilerParams(dimension_semantics=("parallel", "parallel", "arbitrary"),
                                             vmem_limit_bytes=_vmem_limit(est)),
        name="flash_prompt",
    )(qt, k, vt, lq1, lk1, lq2, lk2, sg_col)


def _decode_kernel(pt_ref, q_ref, kn_ref, vn_ref, lq1_ref, lk1_ref, lq2_ref, lk2_ref, sg_ref, *rest,
                   lam_init, hd, heads, n_pages):
    del pt_ref
    k_refs = rest[:n_pages]
    v_refs = rest[n_pages:2 * n_pages]
    o_ref = rest[2 * n_pages]
    maps, hw = q_ref.shape
    page_rows = k_refs[0].shape[0]
    r = lax.broadcasted_iota(jnp.int32, (maps, hw), 0)
    c = lax.broadcasted_iota(jnp.int32, (maps, hw), 1)
    qmat = jnp.where(((r & 1) == 0) == (c < hd), q_ref[...], 0.0)
    qmat_b = qmat.astype(_BF16)
    row_head = lax.broadcasted_iota(jnp.int32, (maps, page_rows), 0) >> 1
    col_head = lax.broadcasted_iota(jnp.int32, (maps, page_rows), 1) & (heads - 1)
    same_head = row_head == col_head

    scores = [jnp.where(same_head, _dot_nt(qmat_b, kr[...].astype(_BF16)), _NEG_INF)
              for kr in k_refs]
    s_new = jnp.sum(qmat * kn_ref[...], axis=1, keepdims=True)
    m = s_new
    for s in scores:
        m = jnp.maximum(m, jnp.max(s, axis=1, keepdims=True))
    p_new = jnp.exp(s_new - m)
    l = p_new
    acc = p_new * vn_ref[...]
    for s, vr in zip(scores, v_refs):
        p = jnp.exp(s - m)
        l = l + jnp.sum(p, axis=1, keepdims=True)
        acc = acc + _dot(p.astype(_BF16), vr[...].astype(_BF16))
    o_all = acc / l

    lam = _lambda_value(lq1_ref, lk1_ref, lq2_ref, lk2_ref, lam_init)
    o = o_all - lam * pltpu.roll(o_all, maps - 1, axis=0)
    o_ref[...] = _rmsnorm(o, sg_ref[...], _SUBLN_EPS) * (1.0 - lam_init)


def _decode_sample(page_table, q, k_new, v_new, pool_k, pool_v, lq1, lk1, lq2, lk2, sg, *, heads, hd,
                   lam_init, page_base):
    n, att_width = q.shape
    n_pages = page_table.shape[1]
    _, page_rows, hw = pool_k.shape
    assert heads & (heads - 1) == 0 and hw == 2 * hd
    maps = 2 * heads
    per_map = lambda a: jnp.repeat(a.reshape(n, heads, hw), 2, axis=1)
    row_spec = pl.BlockSpec((None, maps, hw), lambda b, pt: (b, 0, 0))
    small = [pl.BlockSpec(a.shape, lambda b, pt: (0, 0)) for a in (lq1, lk1, lq2, lk2, sg)]
    page_specs = [pl.BlockSpec((None, page_rows, hw), lambda b, pt, j=j: (pt[b, j] + page_base, 0, 0))
                  for j in range(n_pages)]
    est = 2 * 2 * n_pages * page_rows * hw * 4 + 8 * n_pages * page_rows * hw * 2
    kern = functools.partial(_decode_kernel, lam_init=lam_init, hd=hd, heads=heads, n_pages=n_pages)
    out = pl.pallas_call(
        kern,
        grid_spec=pltpu.PrefetchScalarGridSpec(
            num_scalar_prefetch=1,
            grid=(n,),
            in_specs=[row_spec, row_spec, row_spec] + small + page_specs + page_specs,
            out_specs=row_spec,
        ),
        out_shape=jax.ShapeDtypeStruct((n, maps, hw), _F32),
        compiler_params=pltpu.CompilerParams(dimension_semantics=("arbitrary",),
                                             vmem_limit_bytes=_vmem_limit(est)),
        name="decode_sample",
    )(page_table, per_map(q), per_map(k_new), per_map(v_new), lq1, lk1, lq2, lk2, sg,
      *([pool_k] * n_pages), *([pool_v] * n_pages))
    return out[:, 0::2, :].reshape(n, att_width).astype(_BF16)


def _post_kernel(x_ref, lru_ref, att_ref, p_ref, wo_l_ref, wo_a_ref, g1_ref, g2_ref, wg_ref, wu_ref,
                 wd_ref, g3_ref, wpg_ref, wpp_ref, g4_ref, o_ref):
    tm = x_ref.shape[0]
    n_groups = max(1, tm // _POST_GROUP_ROWS)
    rows = [slice(r * (tm // n_groups), (r + 1) * (tm // n_groups)) for r in range(n_groups)]
    mix = [_dot(lru_ref[r, :], wo_l_ref[...]) + _dot(att_ref[r, :], wo_a_ref[...]) for r in rows]
    x1, gate, up = [], [], []
    for r, mx in zip(rows, mix):
        x = x_ref[r, :] + _rmsnorm(mx, g1_ref[...], _NORM_EPS)
        h = _rmsnorm(x, g2_ref[...], _NORM_EPS).astype(_BF16)
        x1.append(x)
        gate.append(_dot(h, wg_ref[...]))
        up.append(_dot(h, wu_ref[...]))
    f = [_dot((g * jax.nn.sigmoid(g) * u).astype(_BF16), wd_ref[...]) for g, u in zip(gate, up)]
    x2, pg, pp = [], [], []
    for r, x, ff in zip(rows, x1, f):
        x = x + _rmsnorm(ff, g3_ref[...], _NORM_EPS)
        x2.append(x)
        pg.append(_dot(x.astype(_BF16), wpg_ref[...]))
        pp.append(_dot(p_ref[r, :].astype(_BF16), wpp_ref[...]))
    for r, x, a, b in zip(rows, x2, pg, pp):
        o_ref[r, :] = x + _rmsnorm(jax.nn.sigmoid(a) * b, g4_ref[...], _NORM_EPS)


def _post(x, lru, att, p, wo_l, wo_a, g1, g2, wg, wu, wd, g3, wpg, wpp, g4, *, tm):
    n, d = x.shape
    tm = min(tm, n)
    d_ff = wg.shape[1]
    row_spec = lambda w: pl.BlockSpec((tm, w), lambda i: (i, 0))
    weights = (wo_l, wo_a, g1, g2, wg, wu, wd, g3, wpg, wpp, g4)
    est = (sum(2 * w.size for w in weights) + 4 * tm * d * 4 + 2 * tm * (lru.shape[1] + att.shape[1]) * 2
           + 2 * tm * p.shape[1] * 4 + 3 * tm * d_ff * 4 + 6 * tm * d * 4)
    return pl.pallas_call(
        _post_kernel,
        grid=(n // tm,),
        in_specs=[row_spec(d), row_spec(lru.shape[1]), row_spec(att.shape[1]), row_spec(p.shape[1])]
                 + [_const_spec(w.shape) for w in weights],
        out_specs=row_spec(d),
        out_shape=jax.ShapeDtypeStruct((n, d), _F32),
        compiler_params=pltpu.CompilerParams(dimension_semantics=("parallel",),
                                             vmem_limit_bytes=_vmem_limit(est)),
        name="post",
    )(x, lru, att, p, *weights)


def _rope_constants(hd):
    rot = hd // _ROT_FRACTION
    half = rot // 2
    inv = _ROPE_THETA ** (-jnp.arange(0, rot, 2, dtype=_F32) / rot)
    c = np.arange(_LANES) % hd
    inv_lane = jnp.where(jnp.asarray(c < rot), inv[c % half], 0.0)
    sign_lo = np.where(c < half, -1.0, 0.0)
    sign_hi = np.where((c >= half) & (c < rot), 1.0, 0.0)
    rows = [inv_lane, jnp.asarray(sign_lo, _F32), jnp.asarray(sign_hi, _F32)]
    rows += [jnp.zeros((_LANES,), _F32)] * (_SUBLANES - len(rows))
    return jnp.stack(rows).astype(_F32), half


def _block_diag(w):
    blocks, n, _ = w.shape
    eye = jnp.eye(blocks, dtype=w.dtype)
    return jnp.einsum('nij,nm->nimj', w, eye).reshape(blocks * n, blocks * n)


def kernel(x_prompt, x_sample, cache_k, cache_v, state_conv, state_h, page_table, p_prompt, p_sample, norm_pre_mix, w_in, conv_w, conv_b, lru_w_a, lru_b_a, lru_w_x, lru_b_x, lru_L, lambda_q1, lambda_k1, lambda_q2, lambda_k2, subln_g, w_out, norm_post_mix, norm_pre_ffn, ffn_w_gate, ffn_w_up, ffn_w_down, norm_post_ffn, ple_w_gate, ple_w_proj, ple_norm):
    depth = w_in.shape[0]
    bsz, seq, d = x_prompt.shape
    n_dec, dec_seq, _ = x_sample.shape
    assert dec_seq == 1
    _, n_phys, page, heads, hw = cache_k.shape
    hd = hw // 2
    att_width = heads * hw
    width = conv_w.shape[-1]
    n_pages = page_table.shape[1]
    past = n_pages * page
    q_scale = hd ** -0.5
    rope_const, half = _rope_constants(hd)
    row = lambda a: a.reshape(1, -1)
    pool_k = cache_k.reshape(depth * n_phys, page * heads, hw)
    pool_v = cache_v.reshape(depth * n_phys, page * heads, hw)

    xp = x_prompt
    xs = x_sample.reshape(n_dec, d)
    outs = [[] for _ in range(8)]
    for i in range(depth):
        lam_init = 0.8 - 0.6 * math.exp(-0.3 * i)
        win = w_in[i].astype(_BF16)
        wgate = jnp.concatenate([_block_diag(lru_w_a[i]), _block_diag(lru_w_x[i])], axis=1).astype(_BF16)
        bgate = jnp.concatenate([lru_b_a[i].reshape(1, -1), lru_b_x[i].reshape(1, -1)], axis=1)
        pre_args = (row(norm_pre_mix[i]), win, rope_const, conv_w[i], row(conv_b[i]), wgate, bgate,
                    row(lru_L[i]))
        lam_args = (row(lambda_q1[i]), row(lambda_k1[i]), row(lambda_q2[i]), row(lambda_k2[i]),
                    row(subln_g[i]))
        wo = w_out[i].astype(_BF16)
        post_args = (wo[:width], wo[width:], row(norm_post_mix[i]), row(norm_pre_ffn[i]),
                     ffn_w_gate[i].astype(_BF16), ffn_w_up[i].astype(_BF16), ffn_w_down[i].astype(_BF16),
                     row(norm_post_ffn[i]), ple_w_gate[i].astype(_BF16), ple_w_proj[i].astype(_BF16),
                     row(ple_norm[i]))
        dims = dict(width=width, att_width=att_width, half=half)

        qt, kp, vp, kb, vt, lru, cp, hp = _premix_prompt(xp, *pre_args, tm=256,
                                                         q_scale=q_scale * math.log2(math.e), **dims)
        att = _flash_prompt(qt, kb, vt, *lam_args[:4], subln_g[i].reshape(-1, 1), hd=hd, lam_init=lam_init,
                            tk=512)
        xp = _post(xp.reshape(bsz * seq, d), lru.reshape(bsz * seq, width), att.reshape(bsz * seq, att_width),
                   p_prompt[i].reshape(bsz * seq, -1), *post_args, tm=512).reshape(bsz, seq, d)

        qs, ks, vs, lrus, cs, hs = _premix_sample(
            xs, *pre_args, state_conv[i].reshape(n_dec, -1), state_h[i], pos=past, q_scale=q_scale, **dims)
        atts = _decode_sample(page_table, qs, ks, vs, pool_k, pool_v, *lam_args, heads=heads, hd=hd,
                              lam_init=lam_init, page_base=i * n_phys)
        xs = _post(xs, lrus, atts, p_sample[i].reshape(n_dec, -1), *post_args, tm=256)

        for lst, val in zip(outs, (kp.reshape(bsz, seq, heads, hw), vp.reshape(bsz, seq, heads, hw), cp,
                                   hp.reshape(bsz, width), ks.reshape(n_dec, 1, heads, hw),
                                   vs.reshape(n_dec, 1, heads, hw), cs.reshape(n_dec, -1, width), hs)):
            lst.append(val)
    return (xp, xs.reshape(n_dec, 1, d)) + tuple(jnp.stack(o) for o in outs)
```

```python
import functools
import math

import jax
import jax.numpy as jnp
import numpy as np
from jax import lax
from jax.experimental import pallas as pl
from jax.experimental.pallas import tpu as pltpu

_F32 = jnp.float32
_BF16 = jnp.bfloat16

_ROPE_THETA = 500000.0
_ROT_FRACTION = 4
_NEG_INF = -1e30
_NORM_EPS = 1e-6
_SUBLN_EPS = 1e-5
_LRU_LOG_SCALE = 8.0
_POST_GROUP_ROWS = 256

_LANES = 128
_SUBLANES = 8
_MXU_WIDTH = 256
_V7X_VMEM_BYTES = 64 * 1024 * 1024
_VMEM_HEADROOM_BYTES = 8 * 1024 * 1024


def _vmem_limit(estimate_bytes):
    return int(min(max(estimate_bytes, 16 * 1024 * 1024), _V7X_VMEM_BYTES - _VMEM_HEADROOM_BYTES))


def _const_spec(shape):
    return pl.BlockSpec(shape, lambda *_: (0,) * len(shape), pipeline_mode=pl.Buffered(1))


def _dot(a, b):
    return jnp.dot(a, b, preferred_element_type=_F32)


def _dot_nt(a, b):
    return lax.dot_general(a, b, (((1,), (1,)), ((), ())), preferred_element_type=_F32)


def _rmsnorm(x, g, eps):
    ms = jnp.mean(x * x, axis=-1, keepdims=True)
    return x * lax.rsqrt(ms + eps) * g


def _softplus(x):
    return jnp.maximum(x, 0.0) + jnp.log1p(jnp.exp(-jnp.abs(x)))


def _gelu_tanh(x):
    c = math.sqrt(2.0 / math.pi)
    return 0.5 * x * (1.0 + jnp.tanh(c * (x + 0.044715 * (x * x * x))))


def _rope_slab(t, cos, sin_lo, sin_hi, half):
    up = pltpu.roll(t, _LANES - half, axis=1)
    down = pltpu.roll(t, half, axis=1)
    return t * cos + up * sin_lo + down * sin_hi


def _gates_and_coeffs(xc, wgate_ref, bgate_ref, lru_l_ref, width):
    pre = _dot(xc.astype(_BF16), wgate_ref[...]) + bgate_ref[...]
    r = jax.nn.sigmoid(pre[:, :width])
    ig = jax.nn.sigmoid(pre[:, width:])
    log_a = (-_LRU_LOG_SCALE) * r * _softplus(-lru_l_ref[...])
    a = jnp.exp(log_a)
    th = jnp.tanh(log_a)
    var = -2.0 * th / (1.0 - th)
    std = jnp.where(var > 0.0, var * lax.rsqrt(var), 0.0)
    b = std * (ig * xc)
    return a, b


def _premix_prompt_kernel(x_ref, g_ref, win_ref, rope_ref, convw_ref, convb_ref, wgate_ref, bgate_ref,
                          lru_l_ref, qt_ref, k_ref, v_ref, kb_ref, vt_ref, lru_ref, conv_ref, h_ref,
                          xbuf, hcar, hs_buf, cosb, sinb, *, width, att_width, half, q_scale):
    i = pl.program_id(1)
    tm = x_ref.shape[0]
    inv = rope_ref[0:1, :]
    sign_lo = rope_ref[1:2, :]
    sign_hi = rope_ref[2:3, :]

    @pl.when(i == 0)
    def _():
        xbuf[0:_SUBLANES, :] = jnp.zeros((_SUBLANES, width), _F32)
        hcar[...] = jnp.zeros_like(hcar)
        ang = lax.broadcasted_iota(jnp.int32, (tm, _LANES), 0).astype(_F32) * inv
        cosb[...] = jnp.cos(ang)
        sinb[...] = jnp.sin(ang)

    h = _rmsnorm(x_ref[...], g_ref[...], _NORM_EPS).astype(_BF16)
    u = _dot(h, win_ref[...])

    base = (i * tm).astype(_F32) * inv
    cos_a, sin_a = jnp.cos(base), jnp.sin(base)
    cos = cos_a * cosb[...] - sin_a * sinb[...]
    sin = sin_a * cosb[...] + cos_a * sinb[...]
    sin_lo, sin_hi = sin * sign_lo, sin * sign_hi

    q0 = 2 * width
    k0 = q0 + att_width
    v0 = k0 + att_width
    slabs = att_width // _LANES
    for s in range(slabs):
        lo = s * _LANES
        qs = _rope_slab(u[:, q0 + lo:q0 + lo + _LANES], cos, sin_lo, sin_hi, half)
        qt_ref[s] =(qs * q_scale).T.astype(_BF16)
        ks = _rope_slab(u[:, k0 + lo:k0 + lo + _LANES], cos, sin_lo, sin_hi, half)
        vs = u[:, v0 + lo:v0 + lo + _LANES]
        k_ref[pl.ds(s, tm, stride=slabs), :] = ks
        v_ref[pl.ds(s, tm, stride=slabs), :] = vs
        kb_ref[:, lo:lo + _LANES] = ks.astype(_BF16)
        vt_ref[s] = vs.T.astype(_BF16)

    xl = u[:, :width]
    gl = u[:, width:q0]
    xbuf[_SUBLANES:_SUBLANES + tm, :] = xl
    taps = convw_ref.shape[0]
    xc = convb_ref[...] + xl * convw_ref[taps - 1:taps, :]
    for j in range(1, taps):
        xc = xc + xbuf[pl.ds(_SUBLANES - j, tm), :] * convw_ref[taps - 1 - j:taps - j, :]
    tail = xbuf[pl.ds(tm + _SUBLANES - (taps - 1), taps - 1), :]
    xbuf[_SUBLANES - (taps - 1):_SUBLANES, :] = tail

    a, b = _gates_and_coeffs(xc, wgate_ref, bgate_ref, lru_l_ref, width)
    groups = tm // _SUBLANES
    a3 = a.reshape(groups, _SUBLANES, width)
    b3 = b.reshape(groups, _SUBLANES, width)
    row = lax.broadcasted_iota(jnp.int32, (groups, _SUBLANES, width), 1)
    step = 1
    while step < _SUBLANES:
        keep = row >= step
        b3 = jnp.where(keep, a3 * pltpu.roll(b3, step, axis=1) + b3, b3)
        a3 = jnp.where(keep, a3 * pltpu.roll(a3, step, axis=1), a3)
        step *= 2
    carry = hcar[...]
    for gidx in range(groups):
        hs_buf[gidx * _SUBLANES:(gidx + 1) * _SUBLANES, :] = b3[gidx] + a3[gidx] * carry
        carry = a3[gidx, _SUBLANES - 1:_SUBLANES, :] * carry + b3[gidx, _SUBLANES - 1:_SUBLANES, :]
    hcar[...] = carry
    hs = hs_buf[...]
    lru_ref[...] = (hs * _gelu_tanh(gl)).astype(_BF16)

    @pl.when(i == pl.num_programs(1) - 1)
    def _():
        conv_ref[...] = tail
        h_ref[...] = carry


def _premix_prompt(x, g, win, rope_const, convw, convb, wgate, bgate, lru_l, *, width, att_width, half,
                   q_scale, tm):
    bsz, seq, d = x.shape
    in_width = win.shape[1]
    taps = convw.shape[0]
    row_spec = lambda w: pl.BlockSpec((None, tm, w), lambda b, i: (b, i, 0))
    slabs = att_width // _LANES
    head_t_spec = pl.BlockSpec((None, slabs, _LANES, tm), lambda b, i: (b, 0, 0, i))
    out_shape = (
        jax.ShapeDtypeStruct((bsz, slabs, _LANES, seq), _BF16),
        jax.ShapeDtypeStruct((bsz, seq * slabs, _LANES), _F32),
        jax.ShapeDtypeStruct((bsz, seq * slabs, _LANES), _F32),
        jax.ShapeDtypeStruct((bsz, seq, att_width), _BF16),
        jax.ShapeDtypeStruct((bsz, slabs, _LANES, seq), _BF16),
        jax.ShapeDtypeStruct((bsz, seq, width), _BF16),
        jax.ShapeDtypeStruct((bsz, taps - 1, width), _F32),
        jax.ShapeDtypeStruct((bsz, 1, width), _F32),
    )
    est = (2 * win.size + 2 * wgate.size + 2 * tm * d * 4 + 4 * tm * in_width * 4
           + 2 * tm * (3 * att_width * 2 + 2 * att_width * 4 + width * 2) + 6 * tm * width * 4)
    kern = functools.partial(_premix_prompt_kernel, width=width, att_width=att_width, half=half,
                             q_scale=q_scale)
    return pl.pallas_call(
        kern,
        grid=(bsz, seq // tm),
        in_specs=[row_spec(d), _const_spec(g.shape), _const_spec(win.shape), _const_spec(rope_const.shape),
                  _const_spec(convw.shape), _const_spec(convb.shape), _const_spec(wgate.shape),
                  _const_spec(bgate.shape), _const_spec(lru_l.shape)],
        out_specs=(head_t_spec, pl.BlockSpec((None, tm * slabs, _LANES), lambda b, i: (b, i, 0)),
                   pl.BlockSpec((None, tm * slabs, _LANES), lambda b, i: (b, i, 0)), row_spec(att_width),
                   head_t_spec, row_spec(width),
                   pl.BlockSpec((None, taps - 1, width), lambda b, i: (b, 0, 0)),
                   pl.BlockSpec((None, 1, width), lambda b, i: (b, 0, 0))),
        out_shape=out_shape,
        scratch_shapes=[pltpu.VMEM((tm + _SUBLANES, width), _F32), pltpu.VMEM((1, width), _F32),
                        pltpu.VMEM((tm, width), _F32), pltpu.VMEM((tm, _LANES), _F32),
                        pltpu.VMEM((tm, _LANES), _F32)],
        compiler_params=pltpu.CompilerParams(dimension_semantics=("arbitrary", "arbitrary"),
                                             vmem_limit_bytes=_vmem_limit(est)),
        name="premix_prompt",
    )(x, g, win, rope_const, convw, convb, wgate, bgate, lru_l)


def _premix_sample_kernel(x_ref, g_ref, win_ref, rope_ref, convw_ref, convb_ref, wgate_ref, bgate_ref,
                          lru_l_ref, cst_ref, hst_ref, q_ref, k_ref, v_ref, lru_ref, conv_ref, h_ref,
                          *, width, att_width, half, q_scale, pos):
    inv = rope_ref[0:1, :]
    ang = jnp.float32(pos) * inv
    cos, sin = jnp.cos(ang), jnp.sin(ang)
    sin_lo, sin_hi = sin * rope_ref[1:2, :], sin * rope_ref[2:3, :]

    h = _rmsnorm(x_ref[...], g_ref[...], _NORM_EPS).astype(_BF16)
    u = _dot(h, win_ref[...])
    q0 = 2 * width
    k0 = q0 + att_width
    v0 = k0 + att_width
    for s in range(att_width // _LANES):
        lo = s * _LANES
        qs = _rope_slab(u[:, q0 + lo:q0 + lo + _LANES], cos, sin_lo, sin_hi, half)
        q_ref[:, lo:lo + _LANES] = qs * q_scale
        k_ref[:, lo:lo + _LANES] = _rope_slab(u[:, k0 + lo:k0 + lo + _LANES], cos, sin_lo, sin_hi, half)
    v_ref[...] = u[:, v0:v0 + att_width]

    xl = u[:, :width]
    gl = u[:, width:q0]
    taps = convw_ref.shape[0]
    xc = convb_ref[...] + xl * convw_ref[taps - 1:taps, :]
    for j in range(taps - 1):
        xc = xc + cst_ref[:, j * width:(j + 1) * width] * convw_ref[j:j + 1, :]
    for j in range(taps - 2):
        conv_ref[:, j * width:(j + 1) * width] = cst_ref[:, (j + 1) * width:(j + 2) * width]
    conv_ref[:, (taps - 2) * width:(taps - 1) * width] = xl

    a, b = _gates_and_coeffs(xc, wgate_ref, bgate_ref, lru_l_ref, width)
    hs = a * hst_ref[...] + b
    h_ref[...] = hs
    lru_ref[...] = (hs * _gelu_tanh(gl)).astype(_BF16)


def _premix_sample(x, g, win, rope_const, convw, convb, wgate, bgate, lru_l, conv_state, h_state, *,
                   width, att_width, half, q_scale, pos):
    n, d = x.shape
    taps = convw.shape[0]
    out_shape = (
        jax.ShapeDtypeStruct((n, att_width), _F32),
        jax.ShapeDtypeStruct((n, att_width), _F32),
        jax.ShapeDtypeStruct((n, att_width), _F32),
        jax.ShapeDtypeStruct((n, width), _BF16),
        jax.ShapeDtypeStruct((n, (taps - 1) * width), _F32),
        jax.ShapeDtypeStruct((n, width), _F32),
    )
    args = (x, g, win, rope_const, convw, convb, wgate, bgate, lru_l, conv_state, h_state)
    est = 2 * (2 * win.size + 2 * wgate.size) + 8 * n * win.shape[1] * 4
    kern = functools.partial(_premix_sample_kernel, width=width, att_width=att_width, half=half,
                             q_scale=q_scale, pos=pos)
    return pl.pallas_call(
        kern,
        grid=(1,),
        in_specs=[pl.BlockSpec(a.shape, lambda i, nd=a.ndim: (0,) * nd) for a in args],
        out_specs=tuple(pl.BlockSpec(s.shape, lambda i: (0, 0)) for s in out_shape),
        out_shape=out_shape,
        compiler_params=pltpu.CompilerParams(dimension_semantics=("arbitrary",),
                                             vmem_limit_bytes=_vmem_limit(est)),
        name="premix_sample",
    )(*args)


def _lambda_value(lq1_ref, lk1_ref, lq2_ref, lk2_ref, lam_init):
    d1 = jnp.sum(lq1_ref[...] * lk1_ref[...], axis=-1, keepdims=True)
    d2 = jnp.sum(lq2_ref[...] * lk2_ref[...], axis=-1, keepdims=True)
    return jnp.exp(d1) - jnp.exp(d2) + lam_init


def _flash_kernel(qt_ref, k_ref, vt_ref, lq1_ref, lk1_ref, lq2_ref, lk2_ref, sg_ref, o_ref,
                  qs_ref, s0_ref, s1_ref, bm0_ref, bm1_ref, m_ref, l_ref, acc_ref, *, lam_init, hd, chunk):
    qi = pl.program_id(2)
    tq = qt_ref.shape[1]
    tk = s0_ref.shape[0]
    feat = lax.broadcasted_iota(jnp.int32, (_LANES, tq), 0)
    qt = qt_ref[...]
    zero = jnp.zeros_like(qt)
    qs_ref[:, :tq] = jnp.where(feat < hd, qt, zero)
    qs_ref[:, tq:] = jnp.where(feat >= hd, qt, zero)

    m_ref[...] = jnp.full_like(m_ref, _NEG_INF)
    l_ref[...] = jnp.zeros_like(l_ref)
    acc_ref[...] = jnp.zeros_like(acc_ref)

    n_chunks = 2 * tq // chunk
    offsets = [(c * chunk) % tq for c in range(n_chunks)]

    def visible(off, r):
        return r is None or off + chunk - 1 >= r * tk

    def fully_visible(off, r):
        return r is None or off >= r * tk + tk - 1

    def score_chunk(kj, dst, c):
        s_ref, bm_ref = dst
        cols = slice(c * chunk, (c + 1) * chunk)
        s = _dot(kj, qs_ref[:, cols])
        s_ref[:, cols] = s
        bm_ref[:, cols] = jnp.max(s, axis=0, keepdims=True)

    def process_chunk(vtj, src, c, off, r):
        s_ref, bm_ref = src
        cols = slice(c * chunk, (c + 1) * chunk)
        s = s_ref[:, cols]
        if fully_visible(off, r):
            block_max = bm_ref[:, cols]
        else:
            kk = lax.broadcasted_iota(jnp.int32, (tk, chunk), 0) + r * tk
            qq = lax.broadcasted_iota(jnp.int32, (tk, chunk), 1) + off
            s = jnp.where(kk <= qq, s, _NEG_INF)
            block_max = jnp.max(s, axis=0, keepdims=True)
        m_prev = m_ref[:, cols]
        m_next = jnp.maximum(m_prev, block_max)
        p = jnp.exp2(s - m_next)
        alpha = jnp.exp2(m_prev - m_next)
        l_ref[:, cols] = alpha * l_ref[:, cols] + jnp.sum(p, axis=0, keepdims=True)
        acc_ref[:, cols] = alpha * acc_ref[:, cols] + _dot(vtj, p.astype(_BF16))
        m_ref[:, cols] = m_next

    def key_rows(j):
        return pl.ds(pl.multiple_of(j * tk, tk), tk)

    def step(nxt=None, cur=None):
        kj = None if nxt is None else k_ref[key_rows(nxt[0]), :]
        vtj = None if cur is None else vt_ref[:, key_rows(cur[0])]
        for c, off in enumerate(offsets):
            if nxt is not None and visible(off, nxt[2]):
                score_chunk(kj, nxt[1], c)
            if cur is not None and visible(off, cur[2]):
                process_chunk(vtj, cur[1], c, off, cur[2])

    buf0, buf1 = (s0_ref, bm0_ref), (s1_ref, bm1_ref)
    step(nxt=(0, buf0, None))

    def body(jj, carry):
        j = 2 * jj
        step(nxt=(j + 1, buf1, None), cur=(j, buf0, None))
        step(nxt=(j + 2, buf0, None), cur=(j + 1, buf1, None))
        return carry

    lax.fori_loop(0, qi, body, 0)
    step(nxt=(2 * qi + 1, buf1, 1), cur=(2 * qi, buf0, 0))
    step(cur=(2 * qi + 1, buf1, 1))

    lam = _lambda_value(lq1_ref, lk1_ref, lq2_ref, lk2_ref, lam_init)
    o_all = acc_ref[...] * (1.0 / l_ref[...])
    o = o_all[:, :tq] - lam * o_all[:, tq:]
    ms = jnp.mean(o * o, axis=0, keepdims=True)
    o = o * lax.rsqrt(ms + _SUBLN_EPS) * sg_ref[...] * (1.0 - lam_init)
    o_ref[...] = o.T.astype(o_ref.dtype)


def _flash_prompt(qt, k, vt, lq1, lk1, lq2, lk2, sg_col, *, hd, lam_init, tk):
    bsz, heads, hw, seq = qt.shape
    assert hw == _LANES and 2 * hd == hw
    t = 2 * tk
    qt_spec = pl.BlockSpec((None, None, hw, t), lambda b, h, i: (b, h, 0, i))
    k_spec = pl.BlockSpec((None, seq, hw), lambda b, h, i: (b, 0, h))
    vt_spec = pl.BlockSpec((None, None, hw, seq), lambda b, h, i: (b, h, 0, 0))
    o_spec = pl.BlockSpec((None, t, hw), lambda b, h, i: (b, i, h))
    small = [_const_spec(a.shape) for a in (lq1, lk1, lq2, lk2, sg_col)]
    est = 2 * 2 * seq * hw * 2 + 8 * t * hw * 4 + 6 * (2 * t) * tk * 4 + 2 * hw * 2 * t * 4
    kern = functools.partial(_flash_kernel, lam_init=lam_init, hd=hd, chunk=_MXU_WIDTH)
    return pl.pallas_call(
        kern,
        grid=(bsz, heads, seq // t),
        in_specs=[qt_spec, k_spec, vt_spec] + small,
        out_specs=o_spec,
        out_shape=jax.ShapeDtypeStruct((bsz, seq, heads * hw), _BF16),
        scratch_shapes=[pltpu.VMEM((hw, 2 * t), _BF16), pltpu.VMEM((tk, 2 * t), _F32),
                        pltpu.VMEM((tk, 2 * t), _F32), pltpu.VMEM((1, 2 * t), _F32),
                        pltpu.VMEM((1, 2 * t), _F32), pltpu.VMEM((1, 2 * t), _F32),
                        pltpu.VMEM((1, 2 * t), _F32), pltpu.VMEM((hw, 2 * t), _F32)],
        compiler_params=pltpu.CompilerParams(dimension_semantics=("parallel", "parallel", "arbitrary"),
                                             vmem_limit_bytes=_vmem_limit(est)),
        name="flash_prompt",
    )(qt, k, vt, lq1, lk1, lq2, lk2, sg_col)


def _decode_kernel(pt_ref, q_ref, kn_ref, vn_ref, lq1_ref, lk1_ref, lq2_ref, lk2_ref, sg_ref, *rest,
                   lam_init, hd, heads, n_pages):
    del pt_ref
    k_refs = rest[:n_pages]
    v_refs = rest[n_pages:2 * n_pages]
    o_ref = rest[2 * n_pages]
    maps, hw = q_ref.shape
    page_rows = k_refs[0].shape[0]
    r = lax.broadcasted_iota(jnp.int32, (maps, hw), 0)
    c = lax.broadcasted_iota(jnp.int32, (maps, hw), 1)
    qmat = jnp.where(((r & 1) == 0) == (c < hd), q_ref[...], 0.0)
    qmat_b = qmat.astype(_BF16)
    row_head = lax.broadcasted_iota(jnp.int32, (maps, page_rows), 0) >> 1
    col_head = lax.broadcasted_iota(jnp.int32, (maps, page_rows), 1) & (heads - 1)
    same_head = row_head == col_head

    scores = [jnp.where(same_head, _dot_nt(qmat_b, kr[...].astype(_BF16)), _NEG_INF)
              for kr in k_refs]
    s_new = jnp.sum(qmat * kn_ref[...], axis=1, keepdims=True)
    m = s_new
    for s in scores:
        m = jnp.maximum(m, jnp.max(s, axis=1, keepdims=True))
    p_new = jnp.exp(s_new - m)
    l = p_new
    acc = p_new * vn_ref[...]
    for s, vr in zip(scores, v_refs):
        p = jnp.exp(s - m)
        l = l + jnp.sum(p, axis=1, keepdims=True)
        acc = acc + _dot(p.astype(_BF16), vr[...].astype(_BF16))
    o_all = acc / l

    lam = _lambda_value(lq1_ref, lk1_ref, lq2_ref, lk2_ref, lam_init)
    o = o_all - lam * pltpu.roll(o_all, maps - 1, axis=0)
    o_ref[...] = _rmsnorm(o, sg_ref[...], _SUBLN_EPS) * (1.0 - lam_init)


def _decode_sample(page_table, q, k_new, v_new, pool_k, pool_v, lq1, lk1, lq2, lk2, sg, *, heads, hd,
                   lam_init, page_base):
    n, att_width = q.shape
    n_pages = page_table.shape[1]
    _, page_rows, hw = pool_k.shape
    assert heads & (heads - 1) == 0 and hw == 2 * hd
    maps = 2 * heads
    per_map = lambda a: jnp.repeat(a.reshape(n, heads, hw), 2, axis=1)
    row_spec = pl.BlockSpec((None, maps, hw), lambda b, pt: (b, 0, 0))
    small = [pl.BlockSpec(a.shape, lambda b, pt: (0, 0)) for a in (lq1, lk1, lq2, lk2, sg)]
    page_specs = [pl.BlockSpec((None, page_rows, hw), lambda b, pt, j=j: (pt[b, j] + page_base, 0, 0))
                  for j in range(n_pages)]
    est = 2 * 2 * n_pages * page_rows * hw * 4 + 8 * n_pages * page_rows * hw * 2
    kern = functools.partial(_decode_kernel, lam_init=lam_init, hd=hd, heads=heads, n_pages=n_pages)
    out = pl.pallas_call(
        kern,
        grid_spec=pltpu.PrefetchScalarGridSpec(
            num_scalar_prefetch=1,
            grid=(n,),
            in_specs=[row_spec, row_spec, row_spec] + small + page_specs + page_specs,
            out_specs=row_spec,
        ),
        out_shape=jax.ShapeDtypeStruct((n, maps, hw), _F32),
        compiler_params=pltpu.CompilerParams(dimension_semantics=("arbitrary",),
                                             vmem_limit_bytes=_vmem_limit(est)),
        name="decode_sample",
    )(page_table, per_map(q), per_map(k_new), per_map(v_new), lq1, lk1, lq2, lk2, sg,
      *([pool_k] * n_pages), *([pool_v] * n_pages))
    return out[:, 0::2, :].reshape(n, att_width).astype(_BF16)


def _post_kernel(x_ref, lru_ref, att_ref, p_ref, wo_l_ref, wo_a_ref, g1_ref, g2_ref, wg_ref, wu_ref,
                 wd_ref, g3_ref, wpg_ref, wpp_ref, g4_ref, o_ref):
    tm = x_ref.shape[0]
    n_groups = max(1, tm // _POST_GROUP_ROWS)
    rows = [slice(r * (tm // n_groups), (r + 1) * (tm // n_groups)) for r in range(n_groups)]
    mix = [_dot(lru_ref[r, :], wo_l_ref[...]) + _dot(att_ref[r, :], wo_a_ref[...]) for r in rows]
    x1, gate, up = [], [], []
    for r, mx in zip(rows, mix):
        x = x_ref[r, :] + _rmsnorm(mx, g1_ref[...], _NORM_EPS)
        h = _rmsnorm(x, g2_ref[...], _NORM_EPS).astype(_BF16)
        x1.append(x)
        gate.append(_dot(h, wg_ref[...]))
        up.append(_dot(h, wu_ref[...]))
    f = [_dot((g * jax.nn.sigmoid(g) * u).astype(_BF16), wd_ref[...]) for g, u in zip(gate, up)]
    x2, pg, pp = [], [], []
    for r, x, ff in zip(rows, x1, f):
        x = x + _rmsnorm(ff, g3_ref[...], _NORM_EPS)
        x2.append(x)
        pg.append(_dot(x.astype(_BF16), wpg_ref[...]))
        pp.append(_dot(p_ref[r, :].astype(_BF16), wpp_ref[...]))
    for r, x, a, b in zip(rows, x2, pg, pp):
        o_ref[r, :] = x + _rmsnorm(jax.nn.sigmoid(a) * b, g4_ref[...], _NORM_EPS)


def _post(x, lru, att, p, wo_l, wo_a, g1, g2, wg, wu, wd, g3, wpg, wpp, g4, *, tm):
    n, d = x.shape
    tm = min(tm, n)
    d_ff = wg.shape[1]
    row_spec = lambda w: pl.BlockSpec((tm, w), lambda i: (i, 0))
    weights = (wo_l, wo_a, g1, g2, wg, wu, wd, g3, wpg, wpp, g4)
    est = (sum(2 * w.size for w in weights) + 4 * tm * d * 4 + 2 * tm * (lru.shape[1] + att.shape[1]) * 2
           + 2 * tm * p.shape[1] * 4 + 3 * tm * d_ff * 4 + 6 * tm * d * 4)
    return pl.pallas_call(
        _post_kernel,
        grid=(n // tm,),
        in_specs=[row_spec(d), row_spec(lru.shape[1]), row_spec(att.shape[1]), row_spec(p.shape[1])]
                 + [_const_spec(w.shape) for w in weights],
        out_specs=row_spec(d),
        out_shape=jax.ShapeDtypeStruct((n, d), _F32),
        compiler_params=pltpu.CompilerParams(dimension_semantics=("parallel",),
                                             vmem_limit_bytes=_vmem_limit(est)),
        name="post",
    )(x, lru, att, p, *weights)


def _rope_constants(hd):
    rot = hd // _ROT_FRACTION
    half = rot // 2
    inv = _ROPE_THETA ** (-jnp.arange(0, rot, 2, dtype=_F32) / rot)
    c = np.arange(_LANES) % hd
    inv_lane = jnp.where(jnp.asarray(c < rot), inv[c % half], 0.0)
    sign_lo = np.where(c < half, -1.0, 0.0)
    sign_hi = np.where((c >= half) & (c < rot), 1.0, 0.0)
    rows = [inv_lane, jnp.asarray(sign_lo, _F32), jnp.asarray(sign_hi, _F32)]
    rows += [jnp.zeros((_LANES,), _F32)] * (_SUBLANES - len(rows))
    return jnp.stack(rows).astype(_F32), half


def _block_diag(w):
    blocks, n, _ = w.shape
    eye = jnp.eye(blocks, dtype=w.dtype)
    return jnp.einsum('nij,nm->nimj', w, eye).reshape(blocks * n, blocks * n)


def kernel(x_prompt, x_sample, cache_k, cache_v, state_conv, state_h, page_table, p_prompt, p_sample, norm_pre_mix, w_in, conv_w, conv_b, lru_w_a, lru_b_a, lru_w_x, lru_b_x, lru_L, lambda_q1, lambda_k1, lambda_q2, lambda_k2, subln_g, w_out, norm_post_mix, norm_pre_ffn, ffn_w_gate, ffn_w_up, ffn_w_down, norm_post_ffn, ple_w_gate, ple_w_proj, ple_norm):
    depth = w_in.shape[0]
    bsz, seq, d = x_prompt.shape
    n_dec, dec_seq, _ = x_sample.shape
    assert dec_seq == 1
    _, n_phys, page, heads, hw = cache_k.shape
    hd = hw // 2
    att_width = heads * hw
    width = conv_w.shape[-1]
    n_pages = page_table.shape[1]
    past = n_pages * page
    q_scale = hd ** -0.5
    rope_const, half = _rope_constants(hd)
    row = lambda a: a.reshape(1, -1)
    pool_k = cache_k.reshape(depth * n_phys, page * heads, hw)
    pool_v = cache_v.reshape(depth * n_phys, page * heads, hw)

    xp = x_prompt
    xs = x_sample.reshape(n_dec, d)
    outs = [[] for _ in range(8)]
    for i in range(depth):
        lam_init = 0.8 - 0.6 * math.exp(-0.3 * i)
        win = w_in[i].astype(_BF16)
        wgate = jnp.concatenate([_block_diag(lru_w_a[i]), _block_diag(lru_w_x[i])], axis=1).astype(_BF16)
        bgate = jnp.concatenate([lru_b_a[i].reshape(1, -1), lru_b_x[i].reshape(1, -1)], axis=1)
        pre_args = (row(norm_pre_mix[i]), win, rope_const, conv_w[i], row(conv_b[i]), wgate, bgate,
                    row(lru_L[i]))
        lam_args = (row(lambda_q1[i]), row(lambda_k1[i]), row(lambda_q2[i]), row(lambda_k2[i]),
                    row(subln_g[i]))
        wo = w_out[i].astype(_BF16)
        post_args = (wo[:width], wo[width:], row(norm_post_mix[i]), row(norm_pre_ffn[i]),
                     ffn_w_gate[i].astype(_BF16), ffn_w_up[i].astype(_BF16), ffn_w_down[i].astype(_BF16),
                     row(norm_post_ffn[i]), ple_w_gate[i].astype(_BF16), ple_w_proj[i].astype(_BF16),
                     row(ple_norm[i]))
        dims = dict(width=width, att_width=att_width, half=half)

        qt, kp, vp, kb, vt, lru, cp, hp = _premix_prompt(xp, *pre_args, tm=512,
                                                         q_scale=q_scale * math.log2(math.e), **dims)
        att = _flash_prompt(qt, kb, vt, *lam_args[:4], subln_g[i].reshape(-1, 1), hd=hd, lam_init=lam_init,
                            tk=512)
        xp = _post(xp.reshape(bsz * seq, d), lru.reshape(bsz * seq, width), att.reshape(bsz * seq, att_width),
                   p_prompt[i].reshape(bsz * seq, -1), *post_args, tm=512).reshape(bsz, seq, d)

        qs, ks, vs, lrus, cs, hs = _premix_sample(
            xs, *pre_args, state_conv[i].reshape(n_dec, -1), state_h[i], pos=past, q_scale=q_scale, **dims)
        atts = _decode_sample(page_table, qs, ks, vs, pool_k, pool_v, *lam_args, heads=heads, hd=hd,
                              lam_init=lam_init, page_base=i * n_phys)
        xs = _post(xs, lrus, atts, p_sample[i].reshape(n_dec, -1), *post_args, tm=256)

        for lst, val in zip(outs, (kp.reshape(bsz, seq, heads, hw), vp.reshape(bsz, seq, heads, hw), cp,
                                   hp.reshape(bsz, width), ks.reshape(n_dec, 1, heads, hw),
                                   vs.reshape(n_dec, 1, heads, hw), cs.reshape(n_dec, -1, width), hs)):
            lst.append(val)
    return (xp, xs.reshape(n_dec, 1, d)) + tuple(jnp.stack(o) for o in outs)
```

```python
import functools
import math

import jax
import jax.numpy as jnp
import numpy as np
from jax import lax
from jax.experimental import pallas as pl
from jax.experimental.pallas import tpu as pltpu

_F32 = jnp.float32
_BF16 = jnp.bfloat16

_ROPE_THETA = 500000.0
_ROT_FRACTION = 4
_NEG_INF = -1e30
_NORM_EPS = 1e-6
_SUBLN_EPS = 1e-5
_LRU_LOG_SCALE = 8.0
_POST_GROUP_ROWS = 256

_LANES = 128
_SUBLANES = 8
_MXU_WIDTH = 256
_V7X_VMEM_BYTES = 64 * 1024 * 1024
_VMEM_HEADROOM_BYTES = 8 * 1024 * 1024


def _vmem_limit(estimate_bytes):
    return int(min(max(estimate_bytes, 16 * 1024 * 1024), _V7X_VMEM_BYTES - _VMEM_HEADROOM_BYTES))


def _const_spec(shape):
    return pl.BlockSpec(shape, lambda *_: (0,) * len(shape), pipeline_mode=pl.Buffered(1))


def _dot(a, b):
    return jnp.dot(a, b, preferred_element_type=_F32)


def _dot_nt(a, b):
    return lax.dot_general(a, b, (((1,), (1,)), ((), ())), preferred_element_type=_F32)


def _rmsnorm(x, g, eps):
    ms = jnp.mean(x * x, axis=-1, keepdims=True)
    return x * lax.rsqrt(ms + eps) * g


def _softplus(x):
    return jnp.maximum(x, 0.0) + jnp.log1p(jnp.exp(-jnp.abs(x)))


def _gelu_tanh(x):
    c = math.sqrt(2.0 / math.pi)
    return 0.5 * x * (1.0 + jnp.tanh(c * (x + 0.044715 * (x * x * x))))


def _rope_slab(t, cos, sin_lo, sin_hi, half):
    up = pltpu.roll(t, _LANES - half, axis=1)
    down = pltpu.roll(t, half, axis=1)
    return t * cos + up * sin_lo + down * sin_hi


def _gates_and_coeffs(xc, wgate_ref, bgate_ref, lru_l_ref, width):
    pre = _dot(xc.astype(_BF16), wgate_ref[...]) + bgate_ref[...]
    r = jax.nn.sigmoid(pre[:, :width])
    ig = jax.nn.sigmoid(pre[:, width:])
    log_a = (-_LRU_LOG_SCALE) * r * _softplus(-lru_l_ref[...])
    a = jnp.exp(log_a)
    th = jnp.tanh(log_a)
    var = -2.0 * th / (1.0 - th)
    std = jnp.where(var > 0.0, var * lax.rsqrt(var), 0.0)
    b = std * (ig * xc)
    return a, b


def _premix_prompt_kernel(x_ref, g_ref, win_ref, rope_ref, convw_ref, convb_ref, wgate_ref, bgate_ref,
                          lru_l_ref, qt_ref, k_ref, v_ref, kb_ref, vt_ref, lru_ref, conv_ref, h_ref,
                          xbuf, hcar, hs_buf, cosb, sinb, *, width, att_width, half, q_scale):
    i = pl.program_id(1)
    tm = x_ref.shape[0]
    inv = rope_ref[0:1, :]
    sign_lo = rope_ref[1:2, :]
    sign_hi = rope_ref[2:3, :]

    @pl.when(i == 0)
    def _():
        xbuf[0:_SUBLANES, :] = jnp.zeros((_SUBLANES, width), _F32)
        hcar[...] = jnp.zeros_like(hcar)
        ang = lax.broadcasted_iota(jnp.int32, (tm, _LANES), 0).astype(_F32) * inv
        cosb[...] = jnp.cos(ang)
        sinb[...] = jnp.sin(ang)

    h = _rmsnorm(x_ref[...], g_ref[...], _NORM_EPS).astype(_BF16)
    u = _dot(h, win_ref[...])

    base = (i * tm).astype(_F32) * inv
    cos_a, sin_a = jnp.cos(base), jnp.sin(base)
    cos = cos_a * cosb[...] - sin_a * sinb[...]
    sin = sin_a * cosb[...] + cos_a * sinb[...]
    sin_lo, sin_hi = sin * sign_lo, sin * sign_hi

    q0 = 2 * width
    k0 = q0 + att_width
    v0 = k0 + att_width
    slabs = att_width // _LANES
    for s in range(slabs):
        lo = s * _LANES
        qs = _rope_slab(u[:, q0 + lo:q0 + lo + _LANES], cos, sin_lo, sin_hi, half)
        qt_ref[s] =(qs * q_scale).T.astype(_BF16)
        ks = _rope_slab(u[:, k0 + lo:k0 + lo + _LANES], cos, sin_lo, sin_hi, half)
        vs = u[:, v0 + lo:v0 + lo + _LANES]
        k_ref[pl.ds(s, tm, stride=slabs), :] = ks
        v_ref[pl.ds(s, tm, stride=slabs), :] = vs
        kb_ref[:, lo:lo + _LANES] = ks.astype(_BF16)
        vt_ref[s] = vs.T.astype(_BF16)

    xl = u[:, :width]
    gl = u[:, width:q0]
    xbuf[_SUBLANES:_SUBLANES + tm, :] = xl
    taps = convw_ref.shape[0]
    xc = convb_ref[...] + xl * convw_ref[taps - 1:taps, :]
    for j in range(1, taps):
        xc = xc + xbuf[pl.ds(_SUBLANES - j, tm), :] * convw_ref[taps - 1 - j:taps - j, :]
    tail = xbuf[pl.ds(tm + _SUBLANES - (taps - 1), taps - 1), :]
    xbuf[_SUBLANES - (taps - 1):_SUBLANES, :] = tail

    a, b = _gates_and_coeffs(xc, wgate_ref, bgate_ref, lru_l_ref, width)
    groups = tm // _SUBLANES
    a3 = a.reshape(groups, _SUBLANES, width)
    b3 = b.reshape(groups, _SUBLANES, width)
    row = lax.broadcasted_iota(jnp.int32, (groups, _SUBLANES, width), 1)
    step = 1
    while step < _SUBLANES:
        keep = row >= step
        b3 = jnp.where(keep, a3 * pltpu.roll(b3, step, axis=1) + b3, b3)
        a3 = jnp.where(keep, a3 * pltpu.roll(a3, step, axis=1), a3)
        step *= 2
    carry = hcar[...]
    for gidx in range(groups):
        hs_buf[gidx * _SUBLANES:(gidx + 1) * _SUBLANES, :] = b3[gidx] + a3[gidx] * carry
        carry = a3[gidx, _SUBLANES - 1:_SUBLANES, :] * carry + b3[gidx, _SUBLANES - 1:_SUBLANES, :]
    hcar[...] = carry
    hs = hs_buf[...]
    lru_ref[...] = (hs * _gelu_tanh(gl)).astype(_BF16)

    @pl.when(i == pl.num_programs(1) - 1)
    def _():
        conv_ref[...] = tail
        h_ref[...] = carry


def _premix_prompt(x, g, win, rope_const, convw, convb, wgate, bgate, lru_l, *, width, att_width, half,
                   q_scale, tm):
    bsz, seq, d = x.shape
    in_width = win.shape[1]
    taps = convw.shape[0]
    row_spec = lambda w: pl.BlockSpec((None, tm, w), lambda b, i: (b, i, 0))
    slabs = att_width // _LANES
    head_t_spec = pl.BlockSpec((None, slabs, _LANES, tm), lambda b, i: (b, 0, 0, i))
    out_shape = (
        jax.ShapeDtypeStruct((bsz, slabs, _LANES, seq), _BF16),
        jax.ShapeDtypeStruct((bsz, seq * slabs, _LANES), _F32),
        jax.ShapeDtypeStruct((bsz, seq * slabs, _LANES), _F32),
        jax.ShapeDtypeStruct((bsz, seq, att_width), _BF16),
        jax.ShapeDtypeStruct((bsz, slabs, _LANES, seq), _BF16),
        jax.ShapeDtypeStruct((bsz, seq, width), _BF16),
        jax.ShapeDtypeStruct((bsz, taps - 1, width), _F32),
        jax.ShapeDtypeStruct((bsz, 1, width), _F32),
    )
    est = (2 * win.size + 2 * wgate.size + 2 * tm * d * 4 + 4 * tm * in_width * 4
           + 2 * tm * (3 * att_width * 2 + 2 * att_width * 4 + width * 2) + 6 * tm * width * 4)
    kern = functools.partial(_premix_prompt_kernel, width=width, att_width=att_width, half=half,
                             q_scale=q_scale)
    return pl.pallas_call(
        kern,
        grid=(bsz, seq // tm),
        in_specs=[row_spec(d), _const_spec(g.shape), _const_spec(win.shape), _const_spec(rope_const.shape),
                  _const_spec(convw.shape), _const_spec(convb.shape), _const_spec(wgate.shape),
                  _const_spec(bgate.shape), _const_spec(lru_l.shape)],
        out_specs=(head_t_spec, pl.BlockSpec((None, tm * slabs, _LANES), lambda b, i: (b, i, 0)),
                   pl.BlockSpec((None, tm * slabs, _LANES), lambda b, i: (b, i, 0)), row_spec(att_width),
                   head_t_spec, row_spec(width),
                   pl.BlockSpec((None, taps - 1, width), lambda b, i: (b, 0, 0)),
                   pl.BlockSpec((None, 1, width), lambda b, i: (b, 0, 0))),
        out_shape=out_shape,
        scratch_shapes=[pltpu.VMEM((tm + _SUBLANES, width), _F32), pltpu.VMEM((1, width), _F32),
                        pltpu.VMEM((tm, width), _F32), pltpu.VMEM((tm, _LANES), _F32),
                        pltpu.VMEM((tm, _LANES), _F32)],
        compiler_params=pltpu.CompilerParams(dimension_semantics=("arbitrary", "arbitrary"),
                                             vmem_limit_bytes=_vmem_limit(est)),
        name="premix_prompt",
    )(x, g, win, rope_const, convw, convb, wgate, bgate, lru_l)


def _premix_sample_kernel(x_ref, g_ref, win_ref, rope_ref, convw_ref, convb_ref, wgate_ref, bgate_ref,
                          lru_l_ref, cst_ref, hst_ref, q_ref, k_ref, v_ref, lru_ref, conv_ref, h_ref,
                          *, width, att_width, half, q_scale, pos):
    inv = rope_ref[0:1, :]
    ang = jnp.float32(pos) * inv
    cos, sin = jnp.cos(ang), jnp.sin(ang)
    sin_lo, sin_hi = sin * rope_ref[1:2, :], sin * rope_ref[2:3, :]

    h = _rmsnorm(x_ref[...], g_ref[...], _NORM_EPS).astype(_BF16)
    u = _dot(h, win_ref[...])
    q0 = 2 * width
    k0 = q0 + att_width
    v0 = k0 + att_width
    for s in range(att_width // _LANES):
        lo = s * _LANES
        qs = _rope_slab(u[:, q0 + lo:q0 + lo + _LANES], cos, sin_lo, sin_hi, half)
        q_ref[:, lo:lo + _LANES] = qs * q_scale
        k_ref[:, lo:lo + _LANES] = _rope_slab(u[:, k0 + lo:k0 + lo + _LANES], cos, sin_lo, sin_hi, half)
    v_ref[...] = u[:, v0:v0 + att_width]

    xl = u[:, :width]
    gl = u[:, width:q0]
    taps = convw_ref.shape[0]
    xc = convb_ref[...] + xl * convw_ref[taps - 1:taps, :]
    for j in range(taps - 1):
        xc = xc + cst_ref[:, j * width:(j + 1) * width] * convw_ref[j:j + 1, :]
    for j in range(taps - 2):
        conv_ref[:, j * width:(j + 1) * width] = cst_ref[:, (j + 1) * width:(j + 2) * width]
    conv_ref[:, (taps - 2) * width:(taps - 1) * width] = xl

    a, b = _gates_and_coeffs(xc, wgate_ref, bgate_ref, lru_l_ref, width)
    hs = a * hst_ref[...] + b
    h_ref[...] = hs
    lru_ref[...] = (hs * _gelu_tanh(gl)).astype(_BF16)


def _premix_sample(x, g, win, rope_const, convw, convb, wgate, bgate, lru_l, conv_state, h_state, *,
                   width, att_width, half, q_scale, pos):
    n, d = x.shape
    taps = convw.shape[0]
    out_shape = (
        jax.ShapeDtypeStruct((n, att_width), _F32),
        jax.ShapeDtypeStruct((n, att_width), _F32),
        jax.ShapeDtypeStruct((n, att_width), _F32),
        jax.ShapeDtypeStruct((n, width), _BF16),
        jax.ShapeDtypeStruct((n, (taps - 1) * width), _F32),
        jax.ShapeDtypeStruct((n, width), _F32),
    )
    args = (x, g, win, rope_const, convw, convb, wgate, bgate, lru_l, conv_state, h_state)
    est = 2 * (2 * win.size + 2 * wgate.size) + 8 * n * win.shape[1] * 4
    kern = functools.partial(_premix_sample_kernel, width=width, att_width=att_width, half=half,
                             q_scale=q_scale, pos=pos)
    return pl.pallas_call(
        kern,
        grid=(1,),
        in_specs=[pl.BlockSpec(a.shape, lambda i, nd=a.ndim: (0,) * nd) for a in args],
        out_specs=tuple(pl.BlockSpec(s.shape, lambda i: (0, 0)) for s in out_shape),
        out_shape=out_shape,
        compiler_params=pltpu.CompilerParams(dimension_semantics=("arbitrary",),
                                             vmem_limit_bytes=_vmem_limit(est)),
        name="premix_sample",
    )(*args)


def _lambda_value(lq1_ref, lk1_ref, lq2_ref, lk2_ref, lam_init):
    d1 = jnp.sum(lq1_ref[...] * lk1_ref[...], axis=-1, keepdims=True)
    d2 = jnp.sum(lq2_ref[...] * lk2_ref[...], axis=-1, keepdims=True)
    return jnp.exp(d1) - jnp.exp(d2) + lam_init


def _flash_body(qt_ref, k_ref, vt_ref, lam, sg_ref, o_ref,
                qs_ref, s0_ref, s1_ref, bm0_ref, bm1_ref, m_ref, l_ref, acc_ref, *, qi, lam_init, hd, chunk):
    tq = qt_ref.shape[1]
    tk = s0_ref.shape[0]
    feat = lax.broadcasted_iota(jnp.int32, (_LANES, tq), 0)
    qt = qt_ref[...]
    zero = jnp.zeros_like(qt)
    qs_ref[:, :tq] = jnp.where(feat < hd, qt, zero)
    qs_ref[:, tq:] = jnp.where(feat >= hd, qt, zero)

    m_ref[...] = jnp.full_like(m_ref, _NEG_INF)
    l_ref[...] = jnp.zeros_like(l_ref)
    acc_ref[...] = jnp.zeros_like(acc_ref)

    n_chunks = 2 * tq // chunk
    offsets = [(c * chunk) % tq for c in range(n_chunks)]

    def visible(off, r):
        return r is None or off + chunk - 1 >= r * tk

    def fully_visible(off, r):
        return r is None or off >= r * tk + tk - 1

    def score_chunk(kj, dst, c):
        s_ref, bm_ref = dst
        cols = slice(c * chunk, (c + 1) * chunk)
        s = _dot(kj, qs_ref[:, cols])
        s_ref[:, cols] = s
        bm_ref[:, cols] = jnp.max(s, axis=0, keepdims=True)

    def process_chunk(vtj, src, c, off, r):
        s_ref, bm_ref = src
        cols = slice(c * chunk, (c + 1) * chunk)
        s = s_ref[:, cols]
        if fully_visible(off, r):
            block_max = bm_ref[:, cols]
        else:
            kk = lax.broadcasted_iota(jnp.int32, (tk, chunk), 0) + r * tk
            qq = lax.broadcasted_iota(jnp.int32, (tk, chunk), 1) + off
            s = jnp.where(kk <= qq, s, _NEG_INF)
            block_max = jnp.max(s, axis=0, keepdims=True)
        m_prev = m_ref[:, cols]
        m_next = jnp.maximum(m_prev, block_max)
        p = jnp.exp2(s - m_next)
        alpha = jnp.exp2(m_prev - m_next)
        l_ref[:, cols] = alpha * l_ref[:, cols] + jnp.sum(p, axis=0, keepdims=True)
        acc_ref[:, cols] = alpha * acc_ref[:, cols] + _dot(vtj, p.astype(_BF16))
        m_ref[:, cols] = m_next

    def key_rows(j):
        return pl.ds(pl.multiple_of(j * tk, tk), tk)

    def step(nxt=None, cur=None):
        kj = None if nxt is None else k_ref[key_rows(nxt[0]), :]
        vtj = None if cur is None else vt_ref[:, key_rows(cur[0])]
        for c, off in enumerate(offsets):
            if nxt is not None and visible(off, nxt[2]):
                score_chunk(kj, nxt[1], c)
            if cur is not None and visible(off, cur[2]):
                process_chunk(vtj, cur[1], c, off, cur[2])

    buf0, buf1 = (s0_ref, bm0_ref), (s1_ref, bm1_ref)
    step(nxt=(0, buf0, None))

    def body(jj, carry):
        j = 2 * jj
        step(nxt=(j + 1, buf1, None), cur=(j, buf0, None))
        step(nxt=(j + 2, buf0, None), cur=(j + 1, buf1, None))
        return carry

    lax.fori_loop(0, qi, body, 0)
    step(nxt=(2 * qi + 1, buf1, 1), cur=(2 * qi, buf0, 0))
    step(cur=(2 * qi + 1, buf1, 1))

    o_all = acc_ref[...] * (1.0 / l_ref[...])
    o = o_all[:, :tq] - lam * o_all[:, tq:]
    ms = jnp.mean(o * o, axis=0, keepdims=True)
    o = o * lax.rsqrt(ms + _SUBLN_EPS) * sg_ref[...] * (1.0 - lam_init)
    o_ref[...] = o.T.astype(o_ref.dtype)


def _decode_row(q, k_new, v_new, lam, sg, k_refs, v_refs, *, lam_init, hd, heads):
    maps, hw = q.shape
    page_rows = k_refs[0].shape[0]
    r = lax.broadcasted_iota(jnp.int32, (maps, hw), 0)
    c = lax.broadcasted_iota(jnp.int32, (maps, hw), 1)
    qmat = jnp.where(((r & 1) == 0) == (c < hd), q, 0.0)
    qmat_b = qmat.astype(_BF16)
    row_head = lax.broadcasted_iota(jnp.int32, (maps, page_rows), 0) >> 1
    col_head = lax.broadcasted_iota(jnp.int32, (maps, page_rows), 1) & (heads - 1)
    same_head = row_head == col_head

    scores = [jnp.where(same_head, _dot_nt(qmat_b, kr[...].astype(_BF16)), _NEG_INF)
              for kr in k_refs]
    s_new = jnp.sum(qmat * k_new, axis=1, keepdims=True)
    m = s_new
    for s in scores:
        m = jnp.maximum(m, jnp.max(s, axis=1, keepdims=True))
    p_new = jnp.exp(s_new - m)
    l = p_new
    acc = p_new * v_new
    for s, vr in zip(scores, v_refs):
        p = jnp.exp(s - m)
        l = l + jnp.sum(p, axis=1, keepdims=True)
        acc = acc + _dot(p.astype(_BF16), vr[...].astype(_BF16))
    o_all = acc / l
    o = o_all - lam * pltpu.roll(o_all, maps - 1, axis=0)
    return _rmsnorm(o, sg, _SUBLN_EPS) * (1.0 - lam_init)


def _attend_kernel(pt_ref, qt_ref, k_ref, vt_ref, lq1_ref, lk1_ref, lq2_ref, lk2_ref, sg_col_ref, sg_row_ref,
                   qd_ref, knd_ref, vnd_ref, *rest, lam_init, hd, heads, n_pages, rows, chunk):
    del pt_ref
    n_in = 2 * rows * n_pages
    page_refs, (o_ref, od_ref), scratch = rest[:n_in], rest[n_in:n_in + 2], rest[n_in + 2:]
    lam = _lambda_value(lq1_ref, lk1_ref, lq2_ref, lk2_ref, lam_init)
    for r in range(rows):
        k_refs = page_refs[r * n_pages:(r + 1) * n_pages]
        v_refs = page_refs[(rows + r) * n_pages:(rows + r + 1) * n_pages]
        od_ref[r] = _decode_row(qd_ref[r], knd_ref[r], vnd_ref[r], lam, sg_row_ref[...], k_refs, v_refs,
                                lam_init=lam_init, hd=hd, heads=heads)
    _flash_body(qt_ref, k_ref, vt_ref, lam, sg_col_ref, o_ref, *scratch, qi=pl.program_id(2),
                lam_init=lam_init, hd=hd, chunk=chunk)


def _attend(qt, k, vt, page_table, qd, knd, vnd, pool_k, pool_v, lq1, lk1, lq2, lk2, sg, *, heads, hd,
            lam_init, page_base, tk):
    bsz, _, hw, seq = qt.shape
    n, att_width = qd.shape
    n_pages = page_table.shape[1]
    _, page_rows, _ = pool_k.shape
    assert hw == _LANES and 2 * hd == hw and heads & (heads - 1) == 0
    t = 2 * tk
    n_tiles = seq // t
    steps = bsz * heads * n_tiles
    assert n % steps == 0
    rows = n // steps
    maps = 2 * heads
    per_map = lambda a: jnp.repeat(a.reshape(n, heads, hw), 2, axis=1)

    def lin(b, h, i):
        return (b * heads + h) * n_tiles + i

    qt_spec = pl.BlockSpec((None, None, hw, t), lambda b, h, i, pt: (b, h, 0, i))
    k_spec = pl.BlockSpec((None, seq, hw), lambda b, h, i, pt: (b, 0, h))
    vt_spec = pl.BlockSpec((None, None, hw, seq), lambda b, h, i, pt: (b, h, 0, 0))
    o_spec = pl.BlockSpec((None, t, hw), lambda b, h, i, pt: (b, i, h))
    sg_col = sg.reshape(-1, 1)
    small = [pl.BlockSpec(a.shape, lambda b, h, i, pt: (0, 0)) for a in (lq1, lk1, lq2, lk2, sg_col, sg)]
    row_spec = pl.BlockSpec((rows, maps, hw), lambda b, h, i, pt: (lin(b, h, i), 0, 0))
    page_specs = [
        pl.BlockSpec((None, page_rows, hw),
                     lambda b, h, i, pt, r=r, j=j: (pt[lin(b, h, i) * rows + r, j] + page_base, 0, 0))
        for r in range(rows) for j in range(n_pages)]
    est = (2 * 2 * seq * hw * 2 + 8 * t * hw * 4 + 3 * (2 * t) * tk * 4 + 2 * hw * 2 * t * 4
           + 2 * 2 * rows * n_pages * page_rows * hw * 4 + 4 * 1024 * 1024)
    kern = functools.partial(_attend_kernel, lam_init=lam_init, hd=hd, heads=heads, n_pages=n_pages,
                             rows=rows, chunk=_MXU_WIDTH)
    att, out = pl.pallas_call(
        kern,
        grid_spec=pltpu.PrefetchScalarGridSpec(
            num_scalar_prefetch=1,
            grid=(bsz, heads, n_tiles),
            in_specs=[qt_spec, k_spec, vt_spec] + small + [row_spec, row_spec, row_spec]
                     + page_specs + page_specs,
            out_specs=(o_spec, row_spec),
            scratch_shapes=[pltpu.VMEM((hw, 2 * t), _BF16), pltpu.VMEM((tk, 2 * t), _F32),
                            pltpu.VMEM((tk, 2 * t), _F32), pltpu.VMEM((1, 2 * t), _F32),
                            pltpu.VMEM((1, 2 * t), _F32), pltpu.VMEM((1, 2 * t), _F32),
                            pltpu.VMEM((1, 2 * t), _F32), pltpu.VMEM((hw, 2 * t), _F32)],
        ),
        out_shape=(jax.ShapeDtypeStruct((bsz, seq, heads * hw), _BF16),
                   jax.ShapeDtypeStruct((n, maps, hw), _F32)),
        compiler_params=pltpu.CompilerParams(dimension_semantics=("arbitrary", "arbitrary", "arbitrary"),
                                             vmem_limit_bytes=_vmem_limit(est)),
        name="attend",
    )(page_table, qt, k, vt, lq1, lk1, lq2, lk2, sg_col, sg, per_map(qd), per_map(knd), per_map(vnd),
      *([pool_k] * (rows * n_pages)), *([pool_v] * (rows * n_pages)))
    return att, out[:, 0::2, :].reshape(n, att_width).astype(_BF16)


def _post_kernel(x_ref, lru_ref, att_ref, p_ref, wo_l_ref, wo_a_ref, g1_ref, g2_ref, wg_ref, wu_ref,
                 wd_ref, g3_ref, wpg_ref, wpp_ref, g4_ref, o_ref):
    tm = x_ref.shape[0]
    n_groups = max(1, tm // _POST_GROUP_ROWS)
    rows = [slice(r * (tm // n_groups), (r + 1) * (tm // n_groups)) for r in range(n_groups)]
    mix = [_dot(lru_ref[r, :], wo_l_ref[...]) + _dot(att_ref[r, :], wo_a_ref[...]) for r in rows]
    x1, gate, up = [], [], []
    for r, mx in zip(rows, mix):
        x = x_ref[r, :] + _rmsnorm(mx, g1_ref[...], _NORM_EPS)
        h = _rmsnorm(x, g2_ref[...], _NORM_EPS).astype(_BF16)
        x1.append(x)
        gate.append(_dot(h, wg_ref[...]))
        up.append(_dot(h, wu_ref[...]))
    f = [_dot((g * jax.nn.sigmoid(g) * u).astype(_BF16), wd_ref[...]) for g, u in zip(gate, up)]
    x2, pg, pp = [], [], []
    for r, x, ff in zip(rows, x1, f):
        x = x + _rmsnorm(ff, g3_ref[...], _NORM_EPS)
        x2.append(x)
        pg.append(_dot(x.astype(_BF16), wpg_ref[...]))
        pp.append(_dot(p_ref[r, :].astype(_BF16), wpp_ref[...]))
    for r, x, a, b in zip(rows, x2, pg, pp):
        o_ref[r, :] = x + _rmsnorm(jax.nn.sigmoid(a) * b, g4_ref[...], _NORM_EPS)


def _post(x, lru, att, p, wo_l, wo_a, g1, g2, wg, wu, wd, g3, wpg, wpp, g4, *, tm):
    n, d = x.shape
    tm = min(tm, n)
    d_ff = wg.shape[1]
    row_spec = lambda w: pl.BlockSpec((tm, w), lambda i: (i, 0))
    weights = (wo_l, wo_a, g1, g2, wg, wu, wd, g3, wpg, wpp, g4)
    est = (sum(2 * w.size for w in weights) + 4 * tm * d * 4 + 2 * tm * (lru.shape[1] + att.shape[1]) * 2
           + 2 * tm * p.shape[1] * 4 + 3 * tm * d_ff * 4 + 6 * tm * d * 4)
    return pl.pallas_call(
        _post_kernel,
        grid=(n // tm,),
        in_specs=[row_spec(d), row_spec(lru.shape[1]), row_spec(att.shape[1]), row_spec(p.shape[1])]
                 + [_const_spec(w.shape) for w in weights],
        out_specs=row_spec(d),
        out_shape=jax.ShapeDtypeStruct((n, d), _F32),
        compiler_params=pltpu.CompilerParams(dimension_semantics=("parallel",),
                                             vmem_limit_bytes=_vmem_limit(est)),
        name="post",
    )(x, lru, att, p, *weights)


def _rope_constants(hd):
    rot = hd // _ROT_FRACTION
    half = rot // 2
    inv = _ROPE_THETA ** (-jnp.arange(0, rot, 2, dtype=_F32) / rot)
    c = np.arange(_LANES) % hd
    inv_lane = jnp.where(jnp.asarray(c < rot), inv[c % half], 0.0)
    sign_lo = np.where(c < half, -1.0, 0.0)
    sign_hi = np.where((c >= half) & (c < rot), 1.0, 0.0)
    rows = [inv_lane, jnp.asarray(sign_lo, _F32), jnp.asarray(sign_hi, _F32)]
    rows += [jnp.zeros((_LANES,), _F32)] * (_SUBLANES - len(rows))
    return jnp.stack(rows).astype(_F32), half


def _block_diag(w):
    blocks, n, _ = w.shape
    eye = jnp.eye(blocks, dtype=w.dtype)
    return jnp.einsum('nij,nm->nimj', w, eye).reshape(blocks * n, blocks * n)


def kernel(x_prompt, x_sample, cache_k, cache_v, state_conv, state_h, page_table, p_prompt, p_sample, norm_pre_mix, w_in, conv_w, conv_b, lru_w_a, lru_b_a, lru_w_x, lru_b_x, lru_L, lambda_q1, lambda_k1, lambda_q2, lambda_k2, subln_g, w_out, norm_post_mix, norm_pre_ffn, ffn_w_gate, ffn_w_up, ffn_w_down, norm_post_ffn, ple_w_gate, ple_w_proj, ple_norm):
    depth = w_in.shape[0]
    bsz, seq, d = x_prompt.shape
    n_dec, dec_seq, _ = x_sample.shape
    assert dec_seq == 1
    _, n_phys, page, heads, hw = cache_k.shape
    hd = hw // 2
    att_width = heads * hw
    width = conv_w.shape[-1]
    n_pages = page_table.shape[1]
    past = n_pages * page
    q_scale = hd ** -0.5
    rope_const, half = _rope_constants(hd)
    row = lambda a: a.reshape(1, -1)
    pool_k = cache_k.reshape(depth * n_phys, page * heads, hw)
    pool_v = cache_v.reshape(depth * n_phys, page * heads, hw)

    xp = x_prompt
    xs = x_sample.reshape(n_dec, d)
    outs = [[] for _ in range(8)]
    for i in range(depth):
        lam_init = 0.8 - 0.6 * math.exp(-0.3 * i)
        win = w_in[i].astype(_BF16)
        wgate = jnp.concatenate([_block_diag(lru_w_a[i]), _block_diag(lru_w_x[i])], axis=1).astype(_BF16)
        bgate = jnp.concatenate([lru_b_a[i].reshape(1, -1), lru_b_x[i].reshape(1, -1)], axis=1)
        pre_args = (row(norm_pre_mix[i]), win, rope_const, conv_w[i], row(conv_b[i]), wgate, bgate,
                    row(lru_L[i]))
        lam_args = (row(lambda_q1[i]), row(lambda_k1[i]), row(lambda_q2[i]), row(lambda_k2[i]),
                    row(subln_g[i]))
        wo = w_out[i].astype(_BF16)
        post_args = (wo[:width], wo[width:], row(norm_post_mix[i]), row(norm_pre_ffn[i]),
                     ffn_w_gate[i].astype(_BF16), ffn_w_up[i].astype(_BF16), ffn_w_down[i].astype(_BF16),
                     row(norm_post_ffn[i]), ple_w_gate[i].astype(_BF16), ple_w_proj[i].astype(_BF16),
                     row(ple_norm[i]))
        dims = dict(width=width, att_width=att_width, half=half)

        qt, kp, vp, kb, vt, lru, cp, hp = _premix_prompt(xp, *pre_args, tm=512,
                                                         q_scale=q_scale * math.log2(math.e), **dims)
        qs, ks, vs, lrus, cs, hs = _premix_sample(
            xs, *pre_args, state_conv[i].reshape(n_dec, -1), state_h[i], pos=past, q_scale=q_scale, **dims)

        att, atts = _attend(qt, kb, vt, page_table, qs, ks, vs, pool_k, pool_v, *lam_args, heads=heads, hd=hd,
                            lam_init=lam_init, page_base=i * n_phys, tk=512)

        xp = _post(xp.reshape(bsz * seq, d), lru.reshape(bsz * seq, width), att.reshape(bsz * seq, att_width),
                   p_prompt[i].reshape(bsz * seq, -1), *post_args, tm=512).reshape(bsz, seq, d)
        xs = _post(xs, lrus, atts, p_sample[i].reshape(n_dec, -1), *post_args, tm=256)

        for lst, val in zip(outs, (kp.reshape(bsz, seq, heads, hw), vp.reshape(bsz, seq, heads, hw), cp,
                                   hp.reshape(bsz, width), ks.reshape(n_dec, 1, heads, hw),
                                   vs.reshape(n_dec, 1, heads, hw), cs.reshape(n_dec, -1, width), hs)):
            lst.append(val)
    return (xp, xs.reshape(n_dec, 1, d)) + tuple(jnp.stack(o) for o in outs)
```
